```python
import math
import jax, jax.numpy as jnp
from jax import lax
import numpy as np

D_MODEL = 1024
BATCH = 8
SEQ = 4096
DEPTH = 4
DEC_BATCH = 16
DEC_SEQ = 2048
PAST_LEN = 128

HEAD_DIM = 64
N_ATT_HEADS = 8
D_ATT = N_ATT_HEADS * HEAD_DIM
D_CONV = D_MODEL - D_ATT
D_IN = 3 * D_ATT + 2 * D_CONV
CONV_KERNEL = 31
FFN_CONV_KERNEL = 3
D_FF = 2816
DILATED_PATTERNS = ((128, 1), (512, 4), (2048, 16))
N_BUCKETS = 32
REL_MAX_DIST = 1024
EPS = 1e-6
NEG = -1e30
ATT_SCALE = 1.0 / math.sqrt(HEAD_DIM)

kernel_name = "hybrid_dilated_conformer_encoder"


def rms_norm(x, g):
    x32 = x.astype(jnp.float32)
    y = x32 * lax.rsqrt(jnp.mean(x32 * x32, axis=-1, keepdims=True) + EPS)
    return y.astype(x.dtype) * g


def layer_norm(x, g, b):
    x32 = x.astype(jnp.float32)
    mu = jnp.mean(x32, axis=-1, keepdims=True)
    var = jnp.mean(jnp.square(x32 - mu), axis=-1, keepdims=True)
    return ((x32 - mu) * lax.rsqrt(var + EPS)).astype(x.dtype) * g + b


def depthwise_conv(x, w):
    k = w.shape[0]
    pad = k // 2
    return lax.conv_general_dilated(
        x, w.astype(x.dtype)[:, None, :], window_strides=(1,), padding=[(pad, pad)],
        dimension_numbers=("NWC", "WIO", "NWC"), feature_group_count=x.shape[-1])


def t5_bucket_np(rel):
    n = -rel
    half = N_BUCKETS // 2
    ret = (n < 0).astype(np.int32) * half
    n = np.abs(n)
    max_exact = half // 2
    large = max_exact + (np.log(np.maximum(n, 1) / max_exact) / np.log(REL_MAX_DIST / max_exact)
                         * (half - max_exact)).astype(np.int32)
    large = np.minimum(large, half - 1)
    return (ret + np.where(n < max_exact, n, large)).astype(np.int32)


def dilated_branch(q, k, v, rel_bias, window, dilation):
    B, S, H, Dh = q.shape
    r = window // (2 * dilation)
    L = S // dilation
    N = B * dilation

    def to_sub(t):
        return t.reshape(B, L, dilation, H, Dh).transpose(0, 2, 1, 3, 4).reshape(N, L, H, Dh)

    nb = -(-L // r)
    Lp = nb * r
    qb = jnp.pad(to_sub(q), ((0, 0), (0, Lp - L), (0, 0), (0, 0))).reshape(N, nb, r, H, Dh)

    def key_blocks(t):
        tp = jnp.pad(to_sub(t), ((0, 0), (r, Lp - L + r), (0, 0), (0, 0))).reshape(N, nb + 2, r, H, Dh)
        return jnp.concatenate([tp[:, :-2], tp[:, 1:-1], tp[:, 2:]], axis=2)

    kb = key_blocks(k)
    vb = key_blocks(v)

    t_idx = np.arange(r)[:, None]
    u_idx = np.arange(3 * r)[None, :]
    rel = u_idx - r - t_idx
    bias = rel_bias[t5_bucket_np(rel * dilation)]
    blk = np.arange(nb)[:, None, None]
    kpos = blk * r + u_idx[None] - r
    valid = (np.abs(rel)[None] <= r) & (kpos >= 0) & (kpos < L)

    s = jnp.einsum("nbqhd,nbkhd->nbhqk", qb, kb, preferred_element_type=jnp.float32)
    s = s * ATT_SCALE + jnp.transpose(bias, (2, 0, 1)).astype(jnp.float32)
    s = jnp.where(valid[None, :, None], s, NEG)
    m = jnp.max(s, axis=-1, keepdims=True)
    p = jnp.exp(s - m)
    den = jnp.sum(p, axis=-1, keepdims=True)
    o = jnp.einsum("nbhqk,nbkhd->nbqhd", p, vb.astype(jnp.float32)) / jnp.transpose(den, (0, 1, 3, 2, 4))
    lse = jnp.transpose((m + jnp.log(den))[..., 0], (0, 1, 3, 2))

    o = o.reshape(N, Lp, H, Dh)[:, :L].reshape(B, dilation, L, H, Dh).transpose(0, 2, 1, 3, 4).reshape(B, S, H, Dh)
    lse = lse.reshape(N, Lp, H)[:, :L].reshape(B, dilation, L, H).transpose(0, 2, 1, 3).reshape(B, S, H)
    return o, lse


def mixer(h, rel_bias, w_in, q_g, k_g, dw_w, dw_b, ln_g, ln_b, w_out):
    B, S, _ = h.shape
    proj = h @ w_in
    q, k, v, cv, cg = jnp.split(proj, [D_ATT, 2 * D_ATT, 3 * D_ATT, 3 * D_ATT + D_CONV], axis=-1)
    q = rms_norm(q.reshape(B, S, N_ATT_HEADS, HEAD_DIM), q_g)
    k = rms_norm(k.reshape(B, S, N_ATT_HEADS, HEAD_DIM), k_g)
    v = v.reshape(B, S, N_ATT_HEADS, HEAD_DIM)
    outs, lses = [], []
    for window, dilation in DILATED_PATTERNS:
        o, l = dilated_branch(q, k, v, rel_bias, window, dilation)
        outs.append(o)
        lses.append(l)
    wts = jax.nn.softmax(jnp.stack(lses), axis=0)
    att = jnp.sum(wts[..., None] * jnp.stack(outs), axis=0).reshape(B, S, D_ATT).astype(h.dtype)
    u = cv * jax.nn.sigmoid(cg)
    u = depthwise_conv(u, dw_w) + dw_b
    u = jax.nn.silu(layer_norm(u, ln_g, ln_b))
    return jnp.concatenate([att, u], axis=-1) @ w_out


def conv_ffn(h, w_up, dw_w, w_down):
    a, g = jnp.split(h @ w_up, 2, axis=-1)
    g = depthwise_conv(g, dw_w)
    return (a * jax.nn.gelu(g)) @ w_down


def trunk(x, c, rel_bias, norm1_g, norm2_g, w_ada, b_ada, w_in, q_norm_g, k_norm_g,
          conv_dw_w, conv_dw_b, conv_ln_g, conv_ln_b, w_out, w_up, ffn_dw_w, w_down):
    sc = jax.nn.silu(c)
    for l in range(DEPTH):
        mod = (sc @ w_ada[l] + b_ada[l])[:, None, :]
        sh1, s1, g1, sh2, s2, g2 = jnp.split(mod, 6, axis=-1)
        h = rms_norm(x, norm1_g[l]) * (1 + s1) + sh1
        x = x + g1 * mixer(h, rel_bias, w_in[l], q_norm_g[l], k_norm_g[l], conv_dw_w[l],
                           conv_dw_b[l], conv_ln_g[l], conv_ln_b[l], w_out[l])
        h = rms_norm(x, norm2_g[l]) * (1 + s2) + sh2
        x = x + g2 * conv_ffn(h, w_up[l], ffn_dw_w[l], w_down[l])
    return x


def setup_inputs(seed: int = 0) -> dict:
    key = jax.random.key(seed)
    ks = jax.random.split(key, 20)
    f32 = jnp.float32
    nrm = lambda k, shape, s: jax.random.normal(k, shape, f32) * s
    return {
        "x_prompt": nrm(ks[0], (BATCH, SEQ, D_MODEL), 1.0),
        "x_sample": nrm(ks[1], (DEC_BATCH, DEC_SEQ, D_MODEL), 1.0),
        "c_prompt": nrm(ks[2], (BATCH, D_MODEL), 1.0),
        "c_sample": nrm(ks[3], (DEC_BATCH, D_MODEL), 1.0),
        "rel_bias": nrm(ks[4], (N_BUCKETS, N_ATT_HEADS), 0.5),
        "norm1_g": 1.0 + nrm(ks[5], (DEPTH, D_MODEL), 0.05),
        "norm2_g": 1.0 + nrm(ks[6], (DEPTH, D_MODEL), 0.05),
        "w_ada": nrm(ks[7], (DEPTH, D_MODEL, 6 * D_MODEL), 0.5 * D_MODEL ** -0.5),
        "b_ada": nrm(ks[8], (DEPTH, 6 * D_MODEL), 0.02),
        "w_in": nrm(ks[9], (DEPTH, D_MODEL, D_IN), D_MODEL ** -0.5),
        "q_norm_g": 1.0 + nrm(ks[10], (DEPTH, HEAD_DIM), 0.05),
        "k_norm_g": 1.0 + nrm(ks[11], (DEPTH, HEAD_DIM), 0.05),
        "conv_dw_w": nrm(ks[12], (DEPTH, CONV_KERNEL, D_CONV), CONV_KERNEL ** -0.5),
        "conv_dw_b": nrm(ks[13], (DEPTH, D_CONV), 0.02),
        "conv_ln_g": 1.0 + nrm(ks[14], (DEPTH, D_CONV), 0.05),
        "conv_ln_b": nrm(ks[15], (DEPTH, D_CONV), 0.02),
        "w_out": nrm(ks[16], (DEPTH, D_MODEL, D_MODEL), D_MODEL ** -0.5),
        "w_up": nrm(ks[17], (DEPTH, D_MODEL, 2 * D_FF), D_MODEL ** -0.5),
        "ffn_dw_w": nrm(ks[18], (DEPTH, FFN_CONV_KERNEL, D_FF), FFN_CONV_KERNEL ** -0.5),
        "w_down": nrm(ks[19], (DEPTH, D_FF, D_MODEL), D_FF ** -0.5),
    }


def reference(x_prompt, x_sample, c_prompt, c_sample, rel_bias, norm1_g, norm2_g, w_ada, b_ada,
              w_in, q_norm_g, k_norm_g, conv_dw_w, conv_dw_b, conv_ln_g, conv_ln_b, w_out,
              w_up, ffn_dw_w, w_down):
    y_prompt = trunk(x_prompt, c_prompt, rel_bias, norm1_g, norm2_g, w_ada, b_ada, w_in,
                     q_norm_g, k_norm_g, conv_dw_w, conv_dw_b, conv_ln_g, conv_ln_b, w_out,
                     w_up, ffn_dw_w, w_down)
    y_sample = trunk(x_sample, c_sample, rel_bias, norm1_g, norm2_g, w_ada, b_ada, w_in,
                     q_norm_g, k_norm_g, conv_dw_w, conv_dw_b, conv_ln_g, conv_ln_b, w_out,
                     w_up, ffn_dw_w, w_down)
    return (y_prompt, y_sample)
```

```python
import functools
import math

import numpy as np
import jax
import jax.numpy as jnp
from jax import lax
from jax.experimental import pallas as pl
from jax.experimental.pallas import tpu as pltpu

D_MODEL = 1024
HEAD_DIM = 64
N_HEADS = 8
D_ATT = N_HEADS * HEAD_DIM
D_CONV = D_MODEL - D_ATT
D_IN = 3 * D_ATT + 2 * D_CONV
CONV_KERNEL = 31
D_FF = 2816
PATTERNS = ((128, 1), (512, 4), (2048, 16))
SIDE = 64
N_BUCKETS = 32
REL_MAX_DIST = 1024
EPS = 1e-6
NEG = -1e30
ATT_SCALE = 1.0 / math.sqrt(HEAD_DIM)

LANES = 128
SUBLANES = 8
QT = 128
STEP_TOKENS = 2048
HALO = 16
VMEM_LIMIT = 56 * 1024 * 1024

f32 = jnp.float32
bf16 = jnp.bfloat16


def _dot(a, b):
    return jnp.dot(a, b, preferred_element_type=f32)


def _sigmoid(x):
    return 1.0 / (1.0 + jnp.exp(-x))


def _split_bf16(x):
    hi = x.astype(bf16)
    lo = (x - hi.astype(f32)).astype(bf16)
    return hi, lo


def _params(sem):
    return pltpu.CompilerParams(dimension_semantics=sem, vmem_limit_bytes=VMEM_LIMIT)


def _mod_kernel(c_ref, w_ref, b_ref, o_ref):
    c = c_ref[...]
    sc_hi, sc_lo = _split_bf16(c * _sigmoid(c))
    w_hi, w_lo = _split_bf16(w_ref[0])
    acc = _dot(sc_hi, w_hi) + _dot(sc_hi, w_lo) + _dot(sc_lo, w_hi)
    o_ref[0] = acc + b_ref[0]


def _modulation(c_all, w_ada, b_ada):
    depth, _, n_out = w_ada.shape
    rows = c_all.shape[0]
    nb = 1536
    return pl.pallas_call(
        _mod_kernel,
        out_shape=jax.ShapeDtypeStruct((depth, rows, n_out), f32),
        grid=(depth, n_out // nb),
        in_specs=[
            pl.BlockSpec((rows, D_MODEL), lambda l, j: (0, 0)),
            pl.BlockSpec((1, D_MODEL, nb), lambda l, j: (l, 0, j)),
            pl.BlockSpec((1, 1, nb), lambda l, j: (l, 0, j)),
        ],
        out_specs=pl.BlockSpec((1, rows, nb), lambda l, j: (l, 0, j)),
        compiler_params=_params(("arbitrary", "arbitrary")),
        name="adaln_mod",
    )(c_all, w_ada, b_ada.reshape(depth, 1, n_out))


def _inproj_kernel(x_ref, mod_ref, g1_ref, w_ref, e_ref, qg_ref, kg_ref,
                   q1_ref, k1_ref, v1_ref, q4_ref, k4_ref, v4_ref, q16_ref, k16_ref, v16_ref,
                   u_ref, scr, *, ts):
    x = x_ref[0]
    ms = jnp.mean(x * x, axis=-1, keepdims=True)
    shift = mod_ref[0, :, 0:D_MODEL]
    scale = mod_ref[0, :, D_MODEL:2 * D_MODEL]
    h = (x * lax.rsqrt(ms + EPS)) * (g1_ref[...] * (1.0 + scale)) + shift
    proj = _dot(h.astype(bf16), w_ref[...])

    e = e_ref[...]

    def head_norm(t, g):
        hi, lo = _split_bf16(t * t)
        msh = _dot(hi, e) + _dot(lo, e)
        return t * lax.rsqrt(msh + EPS) * g

    q = head_norm(proj[:, 0:D_ATT], qg_ref[...])
    k = head_norm(proj[:, D_ATT:2 * D_ATT], kg_ref[...])
    v = proj[:, 2 * D_ATT:3 * D_ATT]
    cv = proj[:, 3 * D_ATT:3 * D_ATT + D_CONV]
    cg = proj[:, 3 * D_ATT + D_CONV:]
    u_ref[0] = cv * _sigmoid(cg)

    q1_ref[0] = q.astype(bf16)
    k1_ref[0] = k.astype(bf16)
    v1_ref[0] = v.astype(bf16)
    n_slab = D_ATT // LANES
    for a, t in enumerate((q, k, v)):
        for s in range(n_slab):
            scr[a * n_slab + s] = t[:, s * LANES:(s + 1) * LANES]
    for d, refs in ((4, (q4_ref, k4_ref, v4_ref)), (16, (q16_ref, k16_ref, v16_ref))):
        n = ts // d
        for a in range(3):
            for r in range(d):
                for s in range(n_slab):
                    part = scr[a * n_slab + s, pl.ds(r, n, stride=d), :]
                    refs[a][0, r, :, s * LANES:(s + 1) * LANES] = part.astype(bf16)


def _inproj(x, modl, g1, w_in_b, e_mat, qg, kg, ts=512):
    b, s, _ = x.shape
    nt = s // ts
    nat = jax.ShapeDtypeStruct((b, s, D_ATT), bf16)
    out_shape = [nat, nat, nat]
    out_specs = [pl.BlockSpec((1, ts, D_ATT), lambda bi, i: (bi, i, 0))] * 3
    for d in (4, 16):
        out_shape += [jax.ShapeDtypeStruct((b, d, s // d, D_ATT), bf16)] * 3
        out_specs += [pl.BlockSpec((1, d, ts // d, D_ATT), lambda bi, i: (bi, 0, i, 0))] * 3
    out_shape.append(jax.ShapeDtypeStruct((b, s, D_CONV), f32))
    out_specs.append(pl.BlockSpec((1, ts, D_CONV), lambda bi, i: (bi, i, 0)))
    const = lambda bi, i: (0, 0)
    return pl.pallas_call(
        functools.partial(_inproj_kernel, ts=ts),
        out_shape=out_shape,
        grid=(b, nt),
        in_specs=[
            pl.BlockSpec((1, ts, D_MODEL), lambda bi, i: (bi, i, 0)),
            pl.BlockSpec((1, 1, 6 * D_MODEL), lambda bi, i: (bi, 0, 0)),
            pl.BlockSpec((1, D_MODEL), const),
            pl.BlockSpec((D_MODEL, D_IN), const),
            pl.BlockSpec((D_ATT, D_ATT), const),
            pl.BlockSpec((1, D_ATT), const),
            pl.BlockSpec((1, D_ATT), const),
        ],
        out_specs=out_specs,
        scratch_shapes=[pltpu.VMEM((3 * D_ATT // LANES, ts, LANES), f32)],
        compiler_params=_params(("parallel", "parallel")),
        name="inproj",
    )(x, modl, g1, w_in_b, e_mat, qg, kg)


def _attn_kernel(q_ref, k_ref, v_ref, tbl_ref, o_ref, l_ref, *, d, g_tiles, seq, win):
    i = pl.program_id(1)
    n_tiles = seq // QT
    lane = lax.broadcasted_iota(jnp.int32, (1, LANES), 1)
    lo_half = lane < HEAD_DIM

    def body(it, carry):
        r = it // g_tiles
        g = it % g_tiles
        t = i * g_tiles + g
        start = pl.multiple_of(jnp.clip(t * QT - SIDE, 0, seq - win), SIDE)
        var = jnp.where(t == 0, 0, jnp.where(t == n_tiles - 1, 2, 1))
        row0 = pl.multiple_of(g * QT, QT)
        for hp in range(D_ATT // LANES):
            cols = slice(hp * LANES, (hp + 1) * LANES)
            qp = q_ref[0, r, pl.ds(row0, QT), cols]
            kp = k_ref[0, r, pl.ds(start, win), cols]
            vp = v_ref[0, r, pl.ds(start, win), cols]
            pv, mx = [], []
            for e in range(2):
                sel = lo_half if e == 0 else jnp.logical_not(lo_half)
                qm = jnp.where(sel, qp, jnp.zeros_like(qp))
                s = lax.dot_general(qm, kp, (((1,), (1,)), ((), ())), preferred_element_type=f32)
                s = s + tbl_ref[var, 2 * hp + e]
                m = jnp.max(s, axis=-1, keepdims=True)
                p = jnp.exp(s - m).astype(bf16)
                ve = jnp.where(sel, vp, jnp.ones_like(vp))
                pv.append(_dot(p, ve))
                mx.append(m)
            num = jnp.where(lo_half, pv[0], pv[1])
            den = pltpu.roll(jnp.where(lo_half, pv[1], pv[0]), HEAD_DIM, axis=1)
            o = num / den
            lse = jnp.where(lo_half, mx[0], mx[1]) + jnp.log(den)
            if d == 1:
                rows = pl.ds(row0, QT)
            else:
                rows = pl.ds(row0 * d + r, QT, stride=d)
            o_ref[0, hp, rows, :] = o
            l_ref[0, hp, rows, :] = lse
        return carry

    lax.fori_loop(0, d * g_tiles, body, 0)


def _attention(qd, kd, vd, tbl, d):
    b, _, seq, _ = qd.shape
    s_nat = seq * d
    g_tiles = STEP_TOKENS // (QT * d)
    win = tbl.shape[-1]
    n_slab = D_ATT // LANES
    out = jax.ShapeDtypeStruct((b, n_slab, s_nat, LANES), f32)
    out_spec = pl.BlockSpec((1, n_slab, STEP_TOKENS, LANES), lambda bi, i: (bi, 0, i, 0))
    full = pl.BlockSpec((1, d, seq, D_ATT), lambda bi, i: (bi, 0, 0, 0))
    return pl.pallas_call(
        functools.partial(_attn_kernel, d=d, g_tiles=g_tiles, seq=seq, win=win),
        out_shape=[out, out],
        grid=(b, s_nat // STEP_TOKENS),
        in_specs=[
            pl.BlockSpec((1, d, QT * g_tiles, D_ATT), lambda bi, i: (bi, 0, i, 0)),
            full,
            full,
            pl.BlockSpec(tbl.shape, lambda bi, i: (0, 0, 0, 0)),
        ],
        out_specs=[out_spec, out_spec],
        compiler_params=_params(("parallel", "arbitrary")),
        name=f"attn_d{d}",
    )(qd, kd, vd, tbl)


def _t5_bucket(rel):
    n = -rel
    half = N_BUCKETS // 2
    ret = (n < 0).astype(np.int32) * half
    n = np.abs(n)
    max_exact = half // 2
    large = max_exact + (np.log(np.maximum(n, 1) / max_exact) / np.log(REL_MAX_DIST / max_exact)
                         * (half - max_exact)).astype(np.int32)
    large = np.minimum(large, half - 1)
    return (ret + np.where(n < max_exact, n, large)).astype(np.int32)


def _bias_table(rel_bias, d, win):
    t = np.arange(QT)[None, :, None]
    u = np.arange(win)[None, None, :]
    offset = np.array([0, SIDE, 2 * SIDE])[:, None, None]
    rel = u - t - offset
    valid = np.abs(rel) <= SIDE
    idx = _t5_bucket(np.clip(rel, -SIDE, SIDE) * d)
    bias = jnp.transpose(rel_bias[idx], (0, 3, 1, 2)).astype(f32)
    return jnp.where(valid[:, None], bias, NEG)


def _mix_kernel(x_ref, mod_ref, o1_ref, l1_ref, o2_ref, l2_ref, o3_ref, l3_ref,
                u_ref, up_ref, un_ref, dww_ref, dwb_ref, lng_ref, lnb_ref, wout_ref,
                y_ref, uext, cat, *, ts, rc):
    i = pl.program_id(1)
    last = pl.num_programs(1) - 1
    for s in range(D_ATT // LANES):
        la, lb, lc = l1_ref[0, s], l2_ref[0, s], l3_ref[0, s]
        m = jnp.maximum(jnp.maximum(la, lb), lc)
        ea, eb, ec = jnp.exp(la - m), jnp.exp(lb - m), jnp.exp(lc - m)
        att = (ea * o1_ref[0, s] + eb * o2_ref[0, s] + ec * o3_ref[0, s]) / (ea + eb + ec)
        cat[:, s * LANES:(s + 1) * LANES] = att.astype(bf16)

    uext[0, 0:HALO, :] = jnp.where(i > 0, up_ref[0], 0.0)
    uext[0, HALO:HALO + ts, :] = u_ref[0]
    uext[0, HALO + ts:2 * HALO + ts, :] = jnp.where(i < last, un_ref[0], 0.0)
    n_sh = ts + 2 * HALO - SUBLANES
    for sh in range(1, SUBLANES):
        uext[sh, 0:n_sh, :] = uext[0, sh:sh + n_sh, :]
    pad = HALO - CONV_KERNEL // 2

    def chunk(c, carry):
        base = pl.multiple_of(c * rc, rc)
        acc = jnp.zeros((rc, D_CONV), f32) + dwb_ref[...]
        for k in range(CONV_KERNEL):
            off = k + pad
            rows = pl.ds(base + (off // SUBLANES) * SUBLANES, rc)
            acc = acc + dww_ref[k:k + 1, :] * uext[off % SUBLANES, rows, :]
        mu = jnp.mean(acc, axis=-1, keepdims=True)
        xc = acc - mu
        var = jnp.mean(xc * xc, axis=-1, keepdims=True)
        yn = xc * lax.rsqrt(var + EPS) * lng_ref[...] + lnb_ref[...]
        cat[pl.ds(base, rc), D_ATT:] = (yn * _sigmoid(yn)).astype(bf16)
        return carry

    lax.fori_loop(0, ts // rc, chunk, 0)
    y = _dot(cat[...], wout_ref[...])
    gate = mod_ref[0, :, 2 * D_MODEL:3 * D_MODEL]
    y_ref[0] = x_ref[0] + gate * y


def _mix(x, modl, ol, u, dww, dwb, lng, lnb, w_out_b, ts=512, rc=32):
    b, s, _ = x.shape
    nt = s // ts
    hb = ts // HALO
    n_slab = D_ATT // LANES
    slab = pl.BlockSpec((1, n_slab, ts, LANES), lambda bi, i: (bi, 0, i, 0))
    const = lambda bi, i: (0, 0)
    tile = lambda w: pl.BlockSpec((1, ts, w), lambda bi, i: (bi, i, 0))
    return pl.pallas_call(
        functools.partial(_mix_kernel, ts=ts, rc=rc),
        out_shape=jax.ShapeDtypeStruct(x.shape, f32),
        grid=(b, nt),
        in_specs=[
            tile(D_MODEL),
            pl.BlockSpec((1, 1, 6 * D_MODEL), lambda bi, i: (bi, 0, 0)),
            slab, slab, slab, slab, slab, slab,
            tile(D_CONV),
            pl.BlockSpec((1, HALO, D_CONV), lambda bi, i: (bi, jnp.maximum(i * hb - 1, 0), 0)),
            pl.BlockSpec((1, HALO, D_CONV),
                         lambda bi, i: (bi, jnp.minimum((i + 1) * hb, s // HALO - 1), 0)),
            pl.BlockSpec((CONV_KERNEL, D_CONV), const),
            pl.BlockSpec((1, D_CONV), const),
            pl.BlockSpec((1, D_CONV), const),
            pl.BlockSpec((1, D_CONV), const),
            pl.BlockSpec((D_MODEL, D_MODEL), const),
        ],
        out_specs=tile(D_MODEL),
        scratch_shapes=[pltpu.VMEM((SUBLANES, ts + 2 * HALO, D_CONV), f32),
                        pltpu.VMEM((ts, D_MODEL), bf16)],
        compiler_params=_params(("parallel", "parallel")),
        name="mix",
    )(x, modl, *ol, u, u, u, dww, dwb, lng, lnb, w_out_b)


def _ffn_kernel(x_ref, xp_ref, xn_ref, mod_ref, g2_ref, wup_ref, dw_ref, wdn_ref,
                y_ref, hext, gext, z, *, ts, ch):
    i = pl.program_id(1)
    last = pl.num_programs(1) - 1
    shift = mod_ref[0, :, 3 * D_MODEL:4 * D_MODEL]
    scale = mod_ref[0, :, 4 * D_MODEL:5 * D_MODEL]
    gate = mod_ref[0, :, 5 * D_MODEL:6 * D_MODEL]
    a = g2_ref[...] * (1.0 + scale)

    def norm(xx):
        ms = jnp.mean(xx * xx, axis=-1, keepdims=True)
        return (xx * lax.rsqrt(ms + EPS)) * a + shift

    x = x_ref[0]
    hext[0:HALO, :] = jnp.where(i > 0, norm(xp_ref[0]), 0.0).astype(bf16)
    hext[HALO:HALO + ts, :] = norm(x).astype(bf16)
    hext[HALO + ts:2 * HALO + ts, :] = jnp.where(i < last, norm(xn_ref[0]), 0.0).astype(bf16)

    c_gelu = math.sqrt(2.0 / math.pi)
    for j in range(D_FF // ch):
        c0 = j * ch
        gext[...] = _dot(hext[...], wup_ref[:, D_FF + c0:D_FF + c0 + ch])
        gc = (dw_ref[0:1, c0:c0 + ch] * gext[HALO - 1:HALO - 1 + ts, :]
              + dw_ref[1:2, c0:c0 + ch] * gext[HALO:HALO + ts, :]
              + dw_ref[2:3, c0:c0 + ch] * gext[HALO + 1:HALO + 1 + ts, :])
        up = _dot(hext[HALO:HALO + ts, :], wup_ref[:, c0:c0 + ch])
        cdf = 0.5 * (1.0 + jnp.tanh(c_gelu * (gc + 0.044715 * (gc * gc * gc))))
        z[:, c0:c0 + ch] = (up * (gc * cdf)).astype(bf16)
    y = _dot(z[...], wdn_ref[...])
    y_ref[0] = x + gate * y


def _ffn(x, modl, g2, w_up_b, ffn_dw, w_down_b, ts=512, ch=256):
    b, s, _ = x.shape
    nt = s // ts
    hb = ts // HALO
    const = lambda bi, i: (0, 0)
    tile = pl.BlockSpec((1, ts, D_MODEL), lambda bi, i: (bi, i, 0))
    single = pl.Buffered(1)
    return pl.pallas_call(
        functools.partial(_ffn_kernel, ts=ts, ch=ch),
        out_shape=jax.ShapeDtypeStruct(x.shape, f32),
        grid=(b, nt),
        in_specs=[
            tile,
            pl.BlockSpec((1, HALO, D_MODEL), lambda bi, i: (bi, jnp.maximum(i * hb - 1, 0), 0)),
            pl.BlockSpec((1, HALO, D_MODEL),
                         lambda bi, i: (bi, jnp.minimum((i + 1) * hb, s // HALO - 1), 0)),
            pl.BlockSpec((1, 1, 6 * D_MODEL), lambda bi, i: (bi, 0, 0)),
            pl.BlockSpec((1, D_MODEL), const),
            pl.BlockSpec((D_MODEL, 2 * D_FF), const, pipeline_mode=single),
            pl.BlockSpec((3, D_FF), const),
            pl.BlockSpec((D_FF, D_MODEL), const, pipeline_mode=single),
        ],
        out_specs=tile,
        scratch_shapes=[pltpu.VMEM((ts + 2 * HALO, D_MODEL), bf16),
                        pltpu.VMEM((ts + 2 * HALO, ch), f32),
                        pltpu.VMEM((ts, D_FF), bf16)],
        compiler_params=_params(("parallel", "parallel")),
        name="ffn",
    )(x, x, x, modl, g2, w_up_b, ffn_dw, w_down_b)


def _trunk(x, mod, tables, p):
    depth = mod.shape[0]
    for l in range(depth):
        modl = mod[l][:, None, :]
        outs = _inproj(x, modl, p["norm1_g"][l][None], p["w_in"][l], p["e_mat"],
                       p["q_g"][l][None], p["k_g"][l][None])
        u = outs[9]
        ol = []
        for pi, (_, d) in enumerate(PATTERNS):
            q, k, v = outs[3 * pi:3 * pi + 3]
            if d == 1:
                q, k, v = q[:, None], k[:, None], v[:, None]
            ol += _attention(q, k, v, tables[pi], d)
        x = _mix(x, modl, ol, u, p["conv_dw_w"][l], p["conv_dw_b"][l][None],
                 p["conv_ln_g"][l][None], p["conv_ln_b"][l][None], p["w_out"][l])
        x = _ffn(x, modl, p["norm2_g"][l][None], p["w_up"][l], p["ffn_dw_w"][l], p["w_down"][l])
    return x


def kernel(x_prompt, x_sample, c_prompt, c_sample, rel_bias, norm1_g, norm2_g, w_ada, b_ada, w_in, q_norm_g, k_norm_g, conv_dw_w, conv_dw_b, conv_ln_g, conv_ln_b, w_out, w_up, ffn_dw_w, w_down):
    nb = c_prompt.shape[0]
    mod = _modulation(jnp.concatenate([c_prompt, c_sample], axis=0), w_ada, b_ada)
    head_id = np.arange(D_ATT) // HEAD_DIM
    e_mat = jnp.asarray((head_id[:, None] == head_id[None, :]) / HEAD_DIM, dtype=bf16)
    p = dict(
        norm1_g=norm1_g, norm2_g=norm2_g,
        w_in=w_in.astype(bf16), w_out=w_out.astype(bf16),
        w_up=w_up.astype(bf16), w_down=w_down.astype(bf16),
        q_g=jnp.tile(q_norm_g, (1, N_HEADS)) * ATT_SCALE, k_g=jnp.tile(k_norm_g, (1, N_HEADS)),
        conv_dw_w=conv_dw_w, conv_dw_b=conv_dw_b, conv_ln_g=conv_ln_g, conv_ln_b=conv_ln_b,
        ffn_dw_w=ffn_dw_w, e_mat=e_mat,
    )
    ys = []
    for x, modg in ((x_prompt, mod[:, :nb]), (x_sample, mod[:, nb:])):
        seq = x.shape[1]
        tables = [_bias_table(rel_bias, d, min(2 * QT, seq // d)) for _, d in PATTERNS]
        ys.append(_trunk(x, modg, tables, p))
    return tuple(ys)
```

```python
import functools
import math

import numpy as np
import jax
import jax.numpy as jnp
from jax import lax
from jax.experimental import pallas as pl
from jax.experimental.pallas import tpu as pltpu

D_MODEL = 1024
HEAD_DIM = 64
N_HEADS = 8
D_ATT = N_HEADS * HEAD_DIM
D_CONV = D_MODEL - D_ATT
D_IN = 3 * D_ATT + 2 * D_CONV
CONV_KERNEL = 31
D_FF = 2816
PATTERNS = ((128, 1), (512, 4), (2048, 16))
SIDE = 64
N_BUCKETS = 32
REL_MAX_DIST = 1024
EPS = 1e-6
NEG = -1e30
ATT_SCALE = 1.0 / math.sqrt(HEAD_DIM)
LOG2E = 1.0 / math.log(2.0)
LN2 = math.log(2.0)

LANES = 128
SUBLANES = 8
QT = 128
STEP_TOKENS = 2048
HALO = 16
VMEM_LIMIT = 56 * 1024 * 1024

f32 = jnp.float32
bf16 = jnp.bfloat16


def _dot(a, b):
    return jnp.dot(a, b, preferred_element_type=f32)


def _sigmoid(x):
    return 0.5 * jnp.tanh(0.5 * x) + 0.5


def _split_bf16(x):
    hi = x.astype(bf16)
    lo = (x - hi.astype(f32)).astype(bf16)
    return hi, lo


def _params(sem):
    return pltpu.CompilerParams(dimension_semantics=sem, vmem_limit_bytes=VMEM_LIMIT)


def _mod_kernel(c_ref, w_ref, b_ref, o_ref):
    c = c_ref[...]
    sc_hi, sc_lo = _split_bf16(c * _sigmoid(c))
    w_hi, w_lo = _split_bf16(w_ref[0])
    acc = _dot(sc_hi, w_hi) + _dot(sc_hi, w_lo) + _dot(sc_lo, w_hi)
    o_ref[0] = acc + b_ref[0]


def _modulation(c_all, w_ada, b_ada):
    depth, _, n_out = w_ada.shape
    rows = c_all.shape[0]
    nb = 1536
    return pl.pallas_call(
        _mod_kernel,
        out_shape=jax.ShapeDtypeStruct((depth, rows, n_out), f32),
        grid=(depth, n_out // nb),
        in_specs=[
            pl.BlockSpec((rows, D_MODEL), lambda l, j: (0, 0)),
            pl.BlockSpec((1, D_MODEL, nb), lambda l, j: (l, 0, j)),
            pl.BlockSpec((1, 1, nb), lambda l, j: (l, 0, j)),
        ],
        out_specs=pl.BlockSpec((1, rows, nb), lambda l, j: (l, 0, j)),
        compiler_params=_params(("arbitrary", "arbitrary")),
        name="adaln_mod",
    )(c_all, w_ada, b_ada.reshape(depth, 1, n_out))


def _inproj_kernel(x_ref, mod_ref, g1_ref, w_ref, e_ref, qg_ref, kg_ref,
                   q1_ref, k1_ref, v1_ref, q4_ref, k4_ref, v4_ref, q16_ref, k16_ref, v16_ref,
                   u_ref, scr, *, ts):
    x = x_ref[0]
    ms = jnp.mean(x * x, axis=-1, keepdims=True)
    shift = mod_ref[0, :, 0:D_MODEL]
    scale = mod_ref[0, :, D_MODEL:2 * D_MODEL]
    h = (x * lax.rsqrt(ms + EPS)) * (g1_ref[...] * (1.0 + scale)) + shift
    proj = _dot(h.astype(bf16), w_ref[...])

    e = e_ref[...]

    def head_norm(t, g):
        msh = _dot((t * t).astype(bf16), e)
        return t * lax.rsqrt(msh + EPS) * g

    q = head_norm(proj[:, 0:D_ATT], qg_ref[...])
    k = head_norm(proj[:, D_ATT:2 * D_ATT], kg_ref[...])
    v = proj[:, 2 * D_ATT:3 * D_ATT]
    cv = proj[:, 3 * D_ATT:3 * D_ATT + D_CONV]
    cg = proj[:, 3 * D_ATT + D_CONV:]
    u_ref[0] = cv * _sigmoid(cg)

    q1_ref[0] = q.astype(bf16)
    k1_ref[0] = k.astype(bf16)
    v1_ref[0] = v.astype(bf16)
    n_slab = D_ATT // LANES
    for a, t in enumerate((q, k, v)):
        for s in range(n_slab):
            scr[a * n_slab + s] = t[:, s * LANES:(s + 1) * LANES]
    for d, refs in ((4, (q4_ref, k4_ref, v4_ref)), (16, (q16_ref, k16_ref, v16_ref))):
        n = ts // d
        for a in range(3):
            for r in range(d):
                for s in range(n_slab):
                    part = scr[a * n_slab + s, pl.ds(r, n, stride=d), :]
                    refs[a][0, r, :, s * LANES:(s + 1) * LANES] = part.astype(bf16)


def _inproj(x, modl, g1, w_in_b, e_mat, qg, kg, ts=512):
    b, s, _ = x.shape
    nt = s // ts
    nat = jax.ShapeDtypeStruct((b, s, D_ATT), bf16)
    out_shape = [nat, nat, nat]
    out_specs = [pl.BlockSpec((1, ts, D_ATT), lambda bi, i: (bi, i, 0))] * 3
    for d in (4, 16):
        out_shape += [jax.ShapeDtypeStruct((b, d, s // d, D_ATT), bf16)] * 3
        out_specs += [pl.BlockSpec((1, d, ts // d, D_ATT), lambda bi, i: (bi, 0, i, 0))] * 3
    out_shape.append(jax.ShapeDtypeStruct((b, s, D_CONV), f32))
    out_specs.append(pl.BlockSpec((1, ts, D_CONV), lambda bi, i: (bi, i, 0)))
    const = lambda bi, i: (0, 0)
    return pl.pallas_call(
        functools.partial(_inproj_kernel, ts=ts),
        out_shape=out_shape,
        grid=(b, nt),
        in_specs=[
            pl.BlockSpec((1, ts, D_MODEL), lambda bi, i: (bi, i, 0)),
            pl.BlockSpec((1, 1, 6 * D_MODEL), lambda bi, i: (bi, 0, 0)),
            pl.BlockSpec((1, D_MODEL), const),
            pl.BlockSpec((D_MODEL, D_IN), const),
            pl.BlockSpec((D_ATT, D_ATT), const),
            pl.BlockSpec((1, D_ATT), const),
            pl.BlockSpec((1, D_ATT), const),
        ],
        out_specs=out_specs,
        scratch_shapes=[pltpu.VMEM((3 * D_ATT // LANES, ts, LANES), f32)],
        compiler_params=_params(("parallel", "parallel")),
        name="inproj",
    )(x, modl, g1, w_in_b, e_mat, qg, kg)


def _attn_kernel(q_ref, k_ref, v_ref, tbl_ref, o_ref, l_ref, p_a, p_b, m_a, m_b,
                 *, d, g_tiles, seq, win):
    i = pl.program_id(1)
    n_tiles = seq // QT
    n_it = d * g_tiles
    lane = lax.broadcasted_iota(jnp.int32, (1, LANES), 1)
    lo_half = lane < HEAD_DIM
    halves = (lo_half, jnp.logical_not(lo_half))
    n_pair = D_ATT // LANES

    def coords(it):
        r = it // g_tiles
        g = it % g_tiles
        t = i * g_tiles + g
        start = pl.multiple_of(jnp.clip(t * QT - SIDE, 0, seq - win), SIDE)
        var = jnp.where(t == 0, 0, jnp.where(t == n_tiles - 1, 2, 1))
        row0 = pl.multiple_of(g * QT, QT)
        return r, start, var, row0

    def scores(it, p_scr, m_scr):
        r, start, var, row0 = coords(it)
        for hp in range(n_pair):
            cols = slice(hp * LANES, (hp + 1) * LANES)
            qp = q_ref[0, r, pl.ds(row0, QT), cols]
            kp = k_ref[0, r, pl.ds(start, win), cols]
            mx = []
            for e in range(2):
                qm = jnp.where(halves[e], qp, jnp.zeros_like(qp))
                s = lax.dot_general(qm, kp, (((1,), (1,)), ((), ())), preferred_element_type=f32)
                s = s + tbl_ref[var, 2 * hp + e]
                m = jnp.max(s, axis=-1, keepdims=True)
                p_scr[2 * hp + e] = jnp.exp2(s - m).astype(bf16)
                mx.append(m)
            m_scr[hp] = jnp.where(lo_half, mx[0], mx[1])

    def values(it, p_scr, m_scr):
        r, start, var, row0 = coords(it)
        for hp in range(n_pair):
            cols = slice(hp * LANES, (hp + 1) * LANES)
            vp = v_ref[0, r, pl.ds(start, win), cols]
            pv = [_dot(p_scr[2 * hp + e], jnp.where(halves[e], vp, jnp.ones_like(vp)))
                  for e in range(2)]
            num = jnp.where(lo_half, pv[0], pv[1])
            den = pltpu.roll(jnp.where(lo_half, pv[1], pv[0]), HEAD_DIM, axis=1)
            if d == 1:
                rows = pl.ds(row0, QT)
            else:
                rows = pl.ds(row0 * d + r, QT, stride=d)
            o_ref[0, hp, rows, :] = num / den
            l_ref[0, hp, rows, :] = (m_scr[hp] + jnp.log2(den)) * LN2

    scores(0, p_a, m_a)

    def body(j, carry):
        scores(2 * j + 1, p_b, m_b)
        values(2 * j, p_a, m_a)
        scores(2 * j + 2, p_a, m_a)
        values(2 * j + 1, p_b, m_b)
        return carry

    lax.fori_loop(0, n_it // 2 - 1, body, 0)
    scores(n_it - 1, p_b, m_b)
    values(n_it - 2, p_a, m_a)
    values(n_it - 1, p_b, m_b)


def _attention(qd, kd, vd, tbl, d):
    b, _, seq, _ = qd.shape
    s_nat = seq * d
    g_tiles = STEP_TOKENS // (QT * d)
    win = tbl.shape[-1]
    n_slab = D_ATT // LANES
    out = jax.ShapeDtypeStruct((b, n_slab, s_nat, LANES), f32)
    out_spec = pl.BlockSpec((1, n_slab, STEP_TOKENS, LANES), lambda bi, i: (bi, 0, i, 0))
    full = pl.BlockSpec((1, d, seq, D_ATT), lambda bi, i: (bi, 0, 0, 0))
    p_slot = pltpu.VMEM((N_HEADS, QT, win), bf16)
    m_slot = pltpu.VMEM((n_slab, QT, LANES), f32)
    return pl.pallas_call(
        functools.partial(_attn_kernel, d=d, g_tiles=g_tiles, seq=seq, win=win),
        out_shape=[out, out],
        grid=(b, s_nat // STEP_TOKENS),
        in_specs=[
            pl.BlockSpec((1, d, QT * g_tiles, D_ATT), lambda bi, i: (bi, 0, i, 0)),
            full,
            full,
            pl.BlockSpec(tbl.shape, lambda bi, i: (0, 0, 0, 0)),
        ],
        out_specs=[out_spec, out_spec],
        scratch_shapes=[p_slot, p_slot, m_slot, m_slot],
        compiler_params=_params(("parallel", "arbitrary")),
        name=f"attn_d{d}",
    )(qd, kd, vd, tbl)


def _t5_bucket(rel):
    n = -rel
    half = N_BUCKETS // 2
    ret = (n < 0).astype(np.int32) * half
    n = np.abs(n)
    max_exact = half // 2
    large = max_exact + (np.log(np.maximum(n, 1) / max_exact) / np.log(REL_MAX_DIST / max_exact)
                         * (half - max_exact)).astype(np.int32)
    large = np.minimum(large, half - 1)
    return (ret + np.where(n < max_exact, n, large)).astype(np.int32)


def _table_kernel(bias_ref, idx_ref, o_ref):
    h = pl.program_id(1)
    idx = idx_ref[0]
    t = jnp.full(idx.shape, NEG, f32)
    for bucket in range(N_BUCKETS):
        t = jnp.where(idx == bucket, bias_ref[bucket, h], t)
    o_ref[0, 0] = t * LOG2E


def _bias_table(rel_bias, d, win):
    t = np.arange(QT)[None, :, None]
    u = np.arange(win)[None, None, :]
    offset = np.array([0, SIDE, 2 * SIDE])[:, None, None]
    rel = u - t - offset
    idx = np.where(np.abs(rel) <= SIDE, _t5_bucket(np.clip(rel, -SIDE, SIDE) * d), -1)
    return pl.pallas_call(
        _table_kernel,
        out_shape=jax.ShapeDtypeStruct((3, N_HEADS, QT, win), f32),
        grid=(3, N_HEADS),
        in_specs=[
            pl.BlockSpec(memory_space=pltpu.SMEM),
            pl.BlockSpec((1, QT, win), lambda v, h: (v, 0, 0)),
        ],
        out_specs=pl.BlockSpec((1, 1, QT, win), lambda v, h: (v, h, 0, 0)),
        compiler_params=_params(("arbitrary", "arbitrary")),
        name=f"bias_table_d{d}",
    )(rel_bias, jnp.asarray(idx, jnp.int32))


def _mix_kernel(x_ref, mod_ref, o1_ref, l1_ref, o2_ref, l2_ref, o3_ref, l3_ref,
                u_ref, up_ref, un_ref, dww_ref, dwb_ref, lng_ref, lnb_ref, wout_ref,
                y_ref, uext, conv, cat, *, ts, rc):
    i = pl.program_id(1)
    last = pl.num_programs(1) - 1
    for s in range(D_ATT // LANES):
        la, lb, lc = l1_ref[0, s], l2_ref[0, s], l3_ref[0, s]
        m = jnp.maximum(jnp.maximum(la, lb), lc)
        ea, eb, ec = jnp.exp(la - m), jnp.exp(lb - m), jnp.exp(lc - m)
        att = (ea * o1_ref[0, s] + eb * o2_ref[0, s] + ec * o3_ref[0, s]) / (ea + eb + ec)
        cat[:, s * LANES:(s + 1) * LANES] = att.astype(bf16)

    uext[0, 0:HALO, :] = jnp.where(i > 0, up_ref[0], 0.0)
    uext[0, HALO:HALO + ts, :] = u_ref[0]
    uext[0, HALO + ts:2 * HALO + ts, :] = jnp.where(i < last, un_ref[0], 0.0)
    n_sh = ts + 2 * HALO - SUBLANES
    for sh in range(1, SUBLANES):
        uext[sh, 0:n_sh, :] = uext[0, sh:sh + n_sh, :]
    pad = HALO - CONV_KERNEL // 2

    def chunk(c, carry):
        base = pl.multiple_of(c * rc, rc)
        acc = [dwb_ref[...]] * (rc // SUBLANES)
        for k in range(CONV_KERNEL):
            off = k + pad
            w = dww_ref[k]
            row = base + (off // SUBLANES) * SUBLANES
            for g in range(rc // SUBLANES):
                acc[g] = acc[g] + w * uext[off % SUBLANES, pl.ds(row + g * SUBLANES, SUBLANES), :]
        conv[pl.ds(base, rc), :] = jnp.concatenate(acc, axis=0)
        return carry

    lax.fori_loop(0, ts // rc, chunk, 0)
    cv = conv[...]
    mu = jnp.mean(cv, axis=-1, keepdims=True)
    xc = cv - mu
    var = jnp.mean(xc * xc, axis=-1, keepdims=True)
    yn = xc * lax.rsqrt(var + EPS) * lng_ref[...] + lnb_ref[...]
    cat[:, D_ATT:] = (yn * _sigmoid(yn)).astype(bf16)
    y = _dot(cat[...], wout_ref[...])
    gate = mod_ref[0, :, 2 * D_MODEL:3 * D_MODEL]
    y_ref[0] = x_ref[0] + gate * y


def _mix(x, modl, ol, u, dww, dwb, lng, lnb, w_out_b, ts=512, rc=32):
    b, s, _ = x.shape
    nt = s // ts
    hb = ts // HALO
    n_slab = D_ATT // LANES
    slab = pl.BlockSpec((1, n_slab, ts, LANES), lambda bi, i: (bi, 0, i, 0))
    const = lambda bi, i: (0, 0)
    tile = lambda w: pl.BlockSpec((1, ts, w), lambda bi, i: (bi, i, 0))
    return pl.pallas_call(
        functools.partial(_mix_kernel, ts=ts, rc=rc),
        out_shape=jax.ShapeDtypeStruct(x.shape, f32),
        grid=(b, nt),
        in_specs=[
            tile(D_MODEL),
            pl.BlockSpec((1, 1, 6 * D_MODEL), lambda bi, i: (bi, 0, 0)),
            slab, slab, slab, slab, slab, slab,
            tile(D_CONV),
            pl.BlockSpec((1, HALO, D_CONV), lambda bi, i: (bi, jnp.maximum(i * hb - 1, 0), 0)),
            pl.BlockSpec((1, HALO, D_CONV),
                         lambda bi, i: (bi, jnp.minimum((i + 1) * hb, s // HALO - 1), 0)),
            pl.BlockSpec((CONV_KERNEL, SUBLANES, D_CONV), lambda bi, i: (0, 0, 0)),
            pl.BlockSpec((SUBLANES, D_CONV), const),
            pl.BlockSpec((1, D_CONV), const),
            pl.BlockSpec((1, D_CONV), const),
            pl.BlockSpec((D_MODEL, D_MODEL), const),
        ],
        out_specs=tile(D_MODEL),
        scratch_shapes=[pltpu.VMEM((SUBLANES, ts + 2 * HALO, D_CONV), f32),
                        pltpu.VMEM((ts, D_CONV), f32),
                        pltpu.VMEM((ts, D_MODEL), bf16)],
        compiler_params=_params(("parallel", "parallel")),
        name="mix",
    )(x, modl, *ol, u, u, u, dww, dwb, lng, lnb, w_out_b)


def _ffn_kernel(x_ref, xp_ref, xn_ref, mod_ref, g2_ref, wup_ref, dw_ref, wdn_ref,
                y_ref, hext, gext, z, *, ts, ch):
    i = pl.program_id(1)
    last = pl.num_programs(1) - 1
    shift = mod_ref[0, :, 3 * D_MODEL:4 * D_MODEL]
    scale = mod_ref[0, :, 4 * D_MODEL:5 * D_MODEL]
    gate = mod_ref[0, :, 5 * D_MODEL:6 * D_MODEL]
    a = g2_ref[...] * (1.0 + scale)

    def norm(xx):
        ms = jnp.mean(xx * xx, axis=-1, keepdims=True)
        return (xx * lax.rsqrt(ms + EPS)) * a + shift

    x = x_ref[0]
    hext[0:HALO, :] = jnp.where(i > 0, norm(xp_ref[0]), 0.0).astype(bf16)
    hext[HALO:HALO + ts, :] = norm(x).astype(bf16)
    hext[HALO + ts:2 * HALO + ts, :] = jnp.where(i < last, norm(xn_ref[0]), 0.0).astype(bf16)

    c_gelu = math.sqrt(2.0 / math.pi)
    for j in range(D_FF // ch):
        c0 = j * ch
        gext[...] = _dot(hext[...], wup_ref[:, D_FF + c0:D_FF + c0 + ch])
        gc = (dw_ref[0:1, c0:c0 + ch] * gext[HALO - 1:HALO - 1 + ts, :]
              + dw_ref[1:2, c0:c0 + ch] * gext[HALO:HALO + ts, :]
              + dw_ref[2:3, c0:c0 + ch] * gext[HALO + 1:HALO + 1 + ts, :])
        up = _dot(hext[HALO:HALO + ts, :], wup_ref[:, c0:c0 + ch])
        cdf = 0.5 * (1.0 + jnp.tanh(c_gelu * (gc + 0.044715 * (gc * gc * gc))))
        z[:, c0:c0 + ch] = (up * (gc * cdf)).astype(bf16)
    y = _dot(z[...], wdn_ref[...])
    y_ref[0] = x + gate * y


def _ffn(x, modl, g2, w_up_b, ffn_dw, w_down_b, ts=512, ch=256):
    b, s, _ = x.shape
    nt = s // ts
    hb = ts // HALO
    const = lambda bi, i: (0, 0)
    tile = pl.BlockSpec((1, ts, D_MODEL), lambda bi, i: (bi, i, 0))
    single = pl.Buffered(1)
    return pl.pallas_call(
        functools.partial(_ffn_kernel, ts=ts, ch=ch),
        out_shape=jax.ShapeDtypeStruct(x.shape, f32),
        grid=(b, nt),
        in_specs=[
            tile,
            pl.BlockSpec((1, HALO, D_MODEL), lambda bi, i: (bi, jnp.maximum(i * hb - 1, 0), 0)),
            pl.BlockSpec((1, HALO, D_MODEL),
                         lambda bi, i: (bi, jnp.minimum((i + 1) * hb, s // HALO - 1), 0)),
            pl.BlockSpec((1, 1, 6 * D_MODEL), lambda bi, i: (bi, 0, 0)),
            pl.BlockSpec((1, D_MODEL), const),
            pl.BlockSpec((D_MODEL, 2 * D_FF), const, pipeline_mode=single),
            pl.BlockSpec((3, D_FF), const),
            pl.BlockSpec((D_FF, D_MODEL), const, pipeline_mode=single),
        ],
        out_specs=tile,
        scratch_shapes=[pltpu.VMEM((ts + 2 * HALO, D_MODEL), bf16),
                        pltpu.VMEM((ts + 2 * HALO, ch), f32),
                        pltpu.VMEM((ts, D_FF), bf16)],
        compiler_params=_params(("parallel", "parallel")),
        name="ffn",
    )(x, x, x, modl, g2, w_up_b, ffn_dw, w_down_b)


def _trunk(x, mod, tables, p):
    depth = mod.shape[0]
    for l in range(depth):
        modl = mod[l][:, None, :]
        outs = _inproj(x, modl, p["norm1_g"][l][None], p["w_in"][l], p["e_mat"],
                       p["q_g"][l][None], p["k_g"][l][None])
        u = outs[9]
        ol = []
        for pi, (_, d) in enumerate(PATTERNS):
            q, k, v = outs[3 * pi:3 * pi + 3]
            if d == 1:
                q, k, v = q[:, None], k[:, None], v[:, None]
            ol += _attention(q, k, v, tables[pi], d)
        x = _mix(x, modl, ol, u,
                 jnp.broadcast_to(p["conv_dw_w"][l][:, None, :], (CONV_KERNEL, SUBLANES, D_CONV)),
                 jnp.broadcast_to(p["conv_dw_b"][l][None], (SUBLANES, D_CONV)),
                 p["conv_ln_g"][l][None], p["conv_ln_b"][l][None], p["w_out"][l])
        x = _ffn(x, modl, p["norm2_g"][l][None], p["w_up"][l], p["ffn_dw_w"][l], p["w_down"][l])
    return x


def kernel(x_prompt, x_sample, c_prompt, c_sample, rel_bias, norm1_g, norm2_g, w_ada, b_ada, w_in, q_norm_g, k_norm_g, conv_dw_w, conv_dw_b, conv_ln_g, conv_ln_b, w_out, w_up, ffn_dw_w, w_down):
    nb = c_prompt.shape[0]
    mod = _modulation(jnp.concatenate([c_prompt, c_sample], axis=0), w_ada, b_ada)
    head_id = np.arange(D_ATT) // HEAD_DIM
    e_mat = jnp.asarray((head_id[:, None] == head_id[None, :]) / HEAD_DIM, dtype=bf16)
    p = dict(
        norm1_g=norm1_g, norm2_g=norm2_g,
        w_in=w_in.astype(bf16), w_out=w_out.astype(bf16),
        w_up=w_up.astype(bf16), w_down=w_down.astype(bf16),
        q_g=jnp.tile(q_norm_g, (1, N_HEADS)) * (ATT_SCALE * LOG2E), k_g=jnp.tile(k_norm_g, (1, N_HEADS)),
        conv_dw_w=conv_dw_w, conv_dw_b=conv_dw_b, conv_ln_g=conv_ln_g, conv_ln_b=conv_ln_b,
        ffn_dw_w=ffn_dw_w, e_mat=e_mat,
    )
    ys = []
    for x, modg in ((x_prompt, mod[:, :nb]), (x_sample, mod[:, nb:])):
        seq = x.shape[1]
        tables = [_bias_table(rel_bias, d, min(2 * QT, seq // d)) for _, d in PATTERNS]
        ys.append(_trunk(x, modg, tables, p))
    return tuple(ys)
```

```python
import functools
import math

import numpy as np
import jax
import jax.numpy as jnp
from jax import lax
from jax.experimental import pallas as pl
from jax.experimental.pallas import tpu as pltpu

D_MODEL = 1024
HEAD_DIM = 64
N_HEADS = 8
D_ATT = N_HEADS * HEAD_DIM
D_CONV = D_MODEL - D_ATT
D_IN = 3 * D_ATT + 2 * D_CONV
CONV_KERNEL = 31
D_FF = 2816
PATTERNS = ((128, 1), (512, 4), (2048, 16))
SIDE = 64
N_BUCKETS = 32
REL_MAX_DIST = 1024
EPS = 1e-6
NEG = -1e30
ATT_SCALE = 1.0 / math.sqrt(HEAD_DIM)
LOG2E = 1.0 / math.log(2.0)
LN2 = math.log(2.0)

LANES = 128
SUBLANES = 8
QT = 128
STEP_TOKENS = 2048
HALO = 16
VMEM_LIMIT = 56 * 1024 * 1024

f32 = jnp.float32
bf16 = jnp.bfloat16


def _dot(a, b):
    return jnp.dot(a, b, preferred_element_type=f32)


def _sigmoid(x):
    return 0.5 * jnp.tanh(0.5 * x) + 0.5


def _split_bf16(x):
    hi = x.astype(bf16)
    lo = (x - hi.astype(f32)).astype(bf16)
    return hi, lo


def _params(sem):
    return pltpu.CompilerParams(dimension_semantics=sem, vmem_limit_bytes=VMEM_LIMIT)


def _mod_kernel(c_ref, w_ref, b_ref, o_ref):
    c = c_ref[...]
    sc_hi, sc_lo = _split_bf16(c * _sigmoid(c))
    w_hi, w_lo = _split_bf16(w_ref[0])
    acc = _dot(sc_hi, w_hi) + _dot(sc_hi, w_lo) + _dot(sc_lo, w_hi)
    o_ref[0] = acc + b_ref[0]


def _modulation(c_all, w_ada, b_ada):
    depth, _, n_out = w_ada.shape
    rows = c_all.shape[0]
    nb = 1536
    return pl.pallas_call(
        _mod_kernel,
        out_shape=jax.ShapeDtypeStruct((depth, rows, n_out), f32),
        grid=(depth, n_out // nb),
        in_specs=[
            pl.BlockSpec((rows, D_MODEL), lambda l, j: (0, 0)),
            pl.BlockSpec((1, D_MODEL, nb), lambda l, j: (l, 0, j)),
            pl.BlockSpec((1, 1, nb), lambda l, j: (l, 0, j)),
        ],
        out_specs=pl.BlockSpec((1, rows, nb), lambda l, j: (l, 0, j)),
        compiler_params=_params(("arbitrary", "arbitrary")),
        name="adaln_mod",
    )(c_all, w_ada, b_ada.reshape(depth, 1, n_out))


def _inproj_kernel(x_ref, mod_ref, g1_ref, w_ref, e_ref, qg_ref, kg_ref,
                   q1_ref, k1_ref, v1_ref, q4_ref, k4_ref, v4_ref, q16_ref, k16_ref, v16_ref,
                   u_ref, scr, scr4, *, ts):
    x = x_ref[0]
    ms = jnp.mean(x * x, axis=-1, keepdims=True)
    shift = mod_ref[0, :, 0:D_MODEL]
    scale = mod_ref[0, :, D_MODEL:2 * D_MODEL]
    h = (x * lax.rsqrt(ms + EPS)) * (g1_ref[...] * (1.0 + scale)) + shift
    proj = _dot(h.astype(bf16), w_ref[...])

    e = e_ref[...]

    def head_norm(t, g):
        msh = _dot((t * t).astype(bf16), e)
        return t * lax.rsqrt(msh + EPS) * g

    q = head_norm(proj[:, 0:D_ATT], qg_ref[...])
    k = head_norm(proj[:, D_ATT:2 * D_ATT], kg_ref[...])
    v = proj[:, 2 * D_ATT:3 * D_ATT]
    cv = proj[:, 3 * D_ATT:3 * D_ATT + D_CONV]
    cg = proj[:, 3 * D_ATT + D_CONV:]
    u_ref[0] = cv * _sigmoid(cg)

    q1_ref[0] = q.astype(bf16)
    k1_ref[0] = k.astype(bf16)
    v1_ref[0] = v.astype(bf16)
    n_slab = D_ATT // LANES
    for a, t in enumerate((q, k, v)):
        for s in range(n_slab):
            scr[a * n_slab + s] = t[:, s * LANES:(s + 1) * LANES]
    n4, n16 = ts // 4, ts // 16
    for a, (ref4, ref16) in enumerate(((q4_ref, q16_ref), (k4_ref, k16_ref), (v4_ref, v16_ref))):
        for s in range(n_slab):
            cols = slice(s * LANES, (s + 1) * LANES)
            for r4 in range(4):
                part = scr[a * n_slab + s, pl.ds(r4, n4, stride=4), :]
                ref4[0, r4, :, cols] = part.astype(bf16)
                scr4[a * n_slab + s, r4 * n4:(r4 + 1) * n4, :] = part
            for r16 in range(16):
                part = scr4[a * n_slab + s, pl.ds((r16 % 4) * n4 + r16 // 4, n16, stride=4), :]
                ref16[0, r16, :, cols] = part.astype(bf16)


def _inproj(x, modl, g1, w_in_b, e_mat, qg, kg, ts=512):
    b, s, _ = x.shape
    nt = s // ts
    nat = jax.ShapeDtypeStruct((b, s, D_ATT), bf16)
    out_shape = [nat, nat, nat]
    out_specs = [pl.BlockSpec((1, ts, D_ATT), lambda bi, i: (bi, i, 0))] * 3
    for d in (4, 16):
        out_shape += [jax.ShapeDtypeStruct((b, d, s // d, D_ATT), bf16)] * 3
        out_specs += [pl.BlockSpec((1, d, ts // d, D_ATT), lambda bi, i: (bi, 0, i, 0))] * 3
    out_shape.append(jax.ShapeDtypeStruct((b, s, D_CONV), f32))
    out_specs.append(pl.BlockSpec((1, ts, D_CONV), lambda bi, i: (bi, i, 0)))
    const = lambda bi, i: (0, 0)
    return pl.pallas_call(
        functools.partial(_inproj_kernel, ts=ts),
        out_shape=out_shape,
        grid=(b, nt),
        in_specs=[
            pl.BlockSpec((1, ts, D_MODEL), lambda bi, i: (bi, i, 0)),
            pl.BlockSpec((1, 1, 6 * D_MODEL), lambda bi, i: (bi, 0, 0)),
            pl.BlockSpec((1, D_MODEL), const),
            pl.BlockSpec((D_MODEL, D_IN), const),
            pl.BlockSpec((D_ATT, D_ATT), const),
            pl.BlockSpec((1, D_ATT), const),
            pl.BlockSpec((1, D_ATT), const),
        ],
        out_specs=out_specs,
        scratch_shapes=[pltpu.VMEM((3 * D_ATT // LANES, ts, LANES), f32)] * 2,
        compiler_params=_params(("parallel", "parallel")),
        name="inproj",
    )(x, modl, g1, w_in_b, e_mat, qg, kg)


def _attn_kernel(q_ref, k_ref, v_ref, tbl_ref, o_ref, l_ref, p_a, p_b, m_a, m_b,
                 *, d, g_tiles, seq, win):
    i = pl.program_id(1)
    n_tiles = seq // QT
    n_it = d * g_tiles
    lane = lax.broadcasted_iota(jnp.int32, (1, LANES), 1)
    lo_half = lane < HEAD_DIM
    halves = (lo_half, jnp.logical_not(lo_half))
    n_pair = D_ATT // LANES

    def coords(it):
        r = it // g_tiles
        g = it % g_tiles
        t = i * g_tiles + g
        start = pl.multiple_of(jnp.clip(t * QT - SIDE, 0, seq - win), SIDE)
        var = jnp.where(t == 0, 0, jnp.where(t == n_tiles - 1, 2, 1))
        row0 = pl.multiple_of(g * QT, QT)
        return r, start, var, row0

    def scores(it, p_scr, m_scr):
        r, start, var, row0 = coords(it)
        for hp in range(n_pair):
            cols = slice(hp * LANES, (hp + 1) * LANES)
            qp = q_ref[0, r, pl.ds(row0, QT), cols]
            kp = k_ref[0, r, pl.ds(start, win), cols]
            mx = []
            for e in range(2):
                qm = jnp.where(halves[e], qp, jnp.zeros_like(qp))
                s = lax.dot_general(qm, kp, (((1,), (1,)), ((), ())), preferred_element_type=f32)
                s = s + tbl_ref[var, 2 * hp + e]
                m = jnp.max(s, axis=-1, keepdims=True)
                p_scr[2 * hp + e] = jnp.exp2(s - m).astype(bf16)
                mx.append(m)
            m_scr[hp] = jnp.where(lo_half, mx[0], mx[1])

    def values(it, p_scr, m_scr):
        r, start, var, row0 = coords(it)
        for hp in range(n_pair):
            cols = slice(hp * LANES, (hp + 1) * LANES)
            vp = v_ref[0, r, pl.ds(start, win), cols]
            pv = [_dot(p_scr[2 * hp + e], jnp.where(halves[e], vp, jnp.ones_like(vp)))
                  for e in range(2)]
            num = jnp.where(lo_half, pv[0], pv[1])
            den = pltpu.roll(jnp.where(lo_half, pv[1], pv[0]), HEAD_DIM, axis=1)
            if d == 1:
                rows = pl.ds(row0, QT)
            else:
                rows = pl.ds(row0 * d + r, QT, stride=d)
            o_ref[0, hp, rows, :] = num / den
            l_ref[0, hp, rows, :] = (m_scr[hp] + jnp.log2(den)) * LN2

    scores(0, p_a, m_a)

    def body(j, carry):
        scores(2 * j + 1, p_b, m_b)
        values(2 * j, p_a, m_a)
        scores(2 * j + 2, p_a, m_a)
        values(2 * j + 1, p_b, m_b)
        return carry

    lax.fori_loop(0, n_it // 2 - 1, body, 0)
    scores(n_it - 1, p_b, m_b)
    values(n_it - 2, p_a, m_a)
    values(n_it - 1, p_b, m_b)


def _attention(qd, kd, vd, tbl, d):
    b, _, seq, _ = qd.shape
    s_nat = seq * d
    g_tiles = STEP_TOKENS // (QT * d)
    win = tbl.shape[-1]
    n_slab = D_ATT // LANES
    out = jax.ShapeDtypeStruct((b, n_slab, s_nat, LANES), f32)
    out_spec = pl.BlockSpec((1, n_slab, STEP_TOKENS, LANES), lambda bi, i: (bi, 0, i, 0))
    full = pl.BlockSpec((1, d, seq, D_ATT), lambda bi, i: (bi, 0, 0, 0))
    p_slot = pltpu.VMEM((N_HEADS, QT, win), bf16)
    m_slot = pltpu.VMEM((n_slab, QT, LANES), f32)
    return pl.pallas_call(
        functools.partial(_attn_kernel, d=d, g_tiles=g_tiles, seq=seq, win=win),
        out_shape=[out, out],
        grid=(b, s_nat // STEP_TOKENS),
        in_specs=[
            pl.BlockSpec((1, d, QT * g_tiles, D_ATT), lambda bi, i: (bi, 0, i, 0)),
            full,
            full,
            pl.BlockSpec(tbl.shape, lambda bi, i: (0, 0, 0, 0)),
        ],
        out_specs=[out_spec, out_spec],
        scratch_shapes=[p_slot, p_slot, m_slot, m_slot],
        compiler_params=_params(("parallel", "arbitrary")),
        name=f"attn_d{d}",
    )(qd, kd, vd, tbl)


def _t5_bucket(rel):
    n = -rel
    half = N_BUCKETS // 2
    ret = (n < 0).astype(np.int32) * half
    n = np.abs(n)
    max_exact = half // 2
    large = max_exact + (np.log(np.maximum(n, 1) / max_exact) / np.log(REL_MAX_DIST / max_exact)
                         * (half - max_exact)).astype(np.int32)
    large = np.minimum(large, half - 1)
    return (ret + np.where(n < max_exact, n, large)).astype(np.int32)


def _table_kernel(bias_ref, idx_ref, o_ref):
    h = pl.program_id(1)
    idx = idx_ref[0]
    t = jnp.full(idx.shape, NEG, f32)
    for bucket in range(N_BUCKETS):
        t = jnp.where(idx == bucket, bias_ref[bucket, h], t)
    o_ref[0, 0] = t * LOG2E


def _bias_table(rel_bias, d, win):
    t = np.arange(QT)[None, :, None]
    u = np.arange(win)[None, None, :]
    offset = np.array([0, SIDE, 2 * SIDE])[:, None, None]
    rel = u - t - offset
    idx = np.where(np.abs(rel) <= SIDE, _t5_bucket(np.clip(rel, -SIDE, SIDE) * d), -1)
    return pl.pallas_call(
        _table_kernel,
        out_shape=jax.ShapeDtypeStruct((3, N_HEADS, QT, win), f32),
        grid=(3, N_HEADS),
        in_specs=[
            pl.BlockSpec(memory_space=pltpu.SMEM),
            pl.BlockSpec((1, QT, win), lambda v, h: (v, 0, 0)),
        ],
        out_specs=pl.BlockSpec((1, 1, QT, win), lambda v, h: (v, h, 0, 0)),
        compiler_params=_params(("arbitrary", "arbitrary")),
        name=f"bias_table_d{d}",
    )(rel_bias, jnp.asarray(idx, jnp.int32))


N_MIX_IN, N_FFN_IN, N_SCRATCH = 16, 8, 3


def _mix_body(i, last, x_ref, mod_ref, o1_ref, l1_ref, o2_ref, l2_ref, o3_ref, l3_ref,
              u_ref, up_ref, un_ref, dww_ref, dwb_ref, lng_ref, lnb_ref, wout_ref,
              y_ref, uext, conv, cat, *, ts, rc, unroll):
    for s in range(D_ATT // LANES):
        la, lb, lc = l1_ref[0, s], l2_ref[0, s], l3_ref[0, s]
        m = jnp.maximum(jnp.maximum(la, lb), lc)
        ea, eb, ec = jnp.exp(la - m), jnp.exp(lb - m), jnp.exp(lc - m)
        att = (ea * o1_ref[0, s] + eb * o2_ref[0, s] + ec * o3_ref[0, s]) / (ea + eb + ec)
        cat[:, s * LANES:(s + 1) * LANES] = att.astype(bf16)
        yield att[0:1, :]

    uext[0, 0:HALO, :] = jnp.where(i > 0, up_ref[0], 0.0)
    uext[0, HALO:HALO + ts, :] = u_ref[0]
    uext[0, HALO + ts:2 * HALO + ts, :] = jnp.where(i < last, un_ref[0], 0.0)
    n_sh = ts + 2 * HALO - SUBLANES
    for sh in range(1, SUBLANES):
        shifted = uext[0, sh:sh + n_sh, :]
        uext[sh, 0:n_sh, :] = shifted
    yield shifted[0:1, 0:LANES]
    pad = HALO - CONV_KERNEL // 2

    def chunk(base):
        acc = [dwb_ref[...]] * (rc // SUBLANES)
        for k in range(CONV_KERNEL):
            off = k + pad
            w = dww_ref[k]
            row = base + (off // SUBLANES) * SUBLANES
            for g in range(rc // SUBLANES):
                acc[g] = acc[g] + w * uext[off % SUBLANES, pl.ds(row + g * SUBLANES, SUBLANES), :]
        conv[pl.ds(base, rc), :] = jnp.concatenate(acc, axis=0)
        return acc[0][0:1, 0:LANES]

    if unroll:
        for c in range(ts // rc):
            yield chunk(c * rc)
    else:
        def loop_body(c, carry):
            chunk(pl.multiple_of(c * rc, rc))
            return carry

        lax.fori_loop(0, ts // rc, loop_body, 0)
    cv = conv[...]
    mu = jnp.mean(cv, axis=-1, keepdims=True)
    xc = cv - mu
    var = jnp.mean(xc * xc, axis=-1, keepdims=True)
    yn = xc * lax.rsqrt(var + EPS) * lng_ref[...] + lnb_ref[...]
    cat[:, D_ATT:] = (yn * _sigmoid(yn)).astype(bf16)
    yield yn[0:1, 0:LANES]
    y = _dot(cat[...], wout_ref[...])
    gate = mod_ref[0, :, 2 * D_MODEL:3 * D_MODEL]
    y_ref[0] = x_ref[0] + gate * y


def _mix_kernel(*refs, ts, rc):
    for _ in _mix_body(pl.program_id(1), pl.num_programs(1) - 1, *refs, ts=ts, rc=rc, unroll=False):
        pass


def _halo_specs(width, s, ts, at):
    hb = ts // HALO
    prev = pl.BlockSpec((1, HALO, width), at(lambda bi, i: (bi, jnp.maximum(i * hb - 1, 0), 0)))
    nxt = pl.BlockSpec((1, HALO, width),
                       at(lambda bi, i: (bi, jnp.minimum((i + 1) * hb, s // HALO - 1), 0)))
    return prev, nxt


def _mix_io(x, modl, ol, u, dww, dwb, lng, lnb, w_out_b, ts, at):
    s = x.shape[1]
    n_slab = D_ATT // LANES
    slab = pl.BlockSpec((1, n_slab, ts, LANES), at(lambda bi, i: (bi, 0, i, 0)))
    const = at(lambda bi, i: (0, 0))
    tile = lambda w: pl.BlockSpec((1, ts, w), at(lambda bi, i: (bi, i, 0)))
    in_specs = [
        tile(D_MODEL),
        pl.BlockSpec((1, 1, 6 * D_MODEL), at(lambda bi, i: (bi, 0, 0))),
        slab, slab, slab, slab, slab, slab,
        tile(D_CONV),
        *_halo_specs(D_CONV, s, ts, at),
        pl.BlockSpec((CONV_KERNEL, SUBLANES, D_CONV), at(lambda bi, i: (0, 0, 0))),
        pl.BlockSpec((SUBLANES, D_CONV), const),
        pl.BlockSpec((1, D_CONV), const),
        pl.BlockSpec((1, D_CONV), const),
        pl.BlockSpec((D_MODEL, D_MODEL), const, pipeline_mode=pl.Buffered(1)),
    ]
    scratch = [pltpu.VMEM((SUBLANES, ts + 2 * HALO, D_CONV), f32),
               pltpu.VMEM((ts, D_CONV), f32),
               pltpu.VMEM((ts, D_MODEL), bf16)]
    args = (x, modl, *ol, u, u, u, dww, dwb, lng, lnb, w_out_b)
    return args, in_specs, tile(D_MODEL), scratch


def _mix(x, mix_args, ts=512, rc=32):
    b, s, _ = x.shape
    args, in_specs, out_spec, scratch = _mix_io(x, *mix_args, ts, lambda f: f)
    return pl.pallas_call(
        functools.partial(_mix_kernel, ts=ts, rc=rc),
        out_shape=jax.ShapeDtypeStruct(x.shape, f32),
        grid=(b, s // ts),
        in_specs=in_specs,
        out_specs=out_spec,
        scratch_shapes=scratch,
        compiler_params=_params(("parallel", "parallel")),
        name="mix",
    )(*args)


def _ffn_body(i, last, x_ref, xp_ref, xn_ref, mod_ref, g2_ref, wup_ref, dw_ref, wdn_ref,
              y_ref, hext, gext, z, *, ts, ch):
    never = mod_ref[0, :, 0:LANES] > jnp.inf
    shift = mod_ref[0, :, 3 * D_MODEL:4 * D_MODEL]
    scale = mod_ref[0, :, 4 * D_MODEL:5 * D_MODEL]
    gate = mod_ref[0, :, 5 * D_MODEL:6 * D_MODEL]
    a = g2_ref[...] * (1.0 + scale)

    def norm(xx):
        ms = jnp.mean(xx * xx, axis=-1, keepdims=True)
        return (xx * lax.rsqrt(ms + EPS)) * a + shift

    x = x_ref[0]
    hext[0:HALO, :] = jnp.where(i > 0, norm(xp_ref[0]), 0.0).astype(bf16)
    hext[HALO:HALO + ts, :] = norm(x).astype(bf16)
    hext[HALO + ts:2 * HALO + ts, :] = jnp.where(i < last, norm(xn_ref[0]), 0.0).astype(bf16)

    c_gelu = math.sqrt(2.0 / math.pi)
    for j in range(D_FF // ch):
        c0 = j * ch
        tie = yield
        w_mid = dw_ref[1:2, c0:c0 + ch]
        if tie is not None:
            w_mid = jnp.concatenate(
                [jnp.where(never, tie, w_mid[:, 0:LANES]), w_mid[:, LANES:]], axis=1)
        gext[...] = _dot(hext[...], wup_ref[:, D_FF + c0:D_FF + c0 + ch])
        gc = (dw_ref[0:1, c0:c0 + ch] * gext[HALO - 1:HALO - 1 + ts, :]
              + w_mid * gext[HALO:HALO + ts, :]
              + dw_ref[2:3, c0:c0 + ch] * gext[HALO + 1:HALO + 1 + ts, :])
        up = _dot(hext[HALO:HALO + ts, :], wup_ref[:, c0:c0 + ch])
        cdf = 0.5 * (1.0 + jnp.tanh(c_gelu * (gc + 0.044715 * (gc * gc * gc))))
        z[:, c0:c0 + ch] = (up * (gc * cdf)).astype(bf16)
    tie = yield
    if tie is not None:
        gate = jnp.concatenate(
            [jnp.where(never, tie, gate[:, 0:LANES]), gate[:, LANES:]], axis=1)
    y = _dot(z[...], wdn_ref[...])
    y_ref[0] = x + gate * y


def _ffn_kernel(*refs, ts, ch):
    for _ in _ffn_body(pl.program_id(1), pl.num_programs(1) - 1, *refs, ts=ts, ch=ch):
        pass


def _ffn_io(x, modl, g2, w_up_b, ffn_dw, w_down_b, ts, ch, at):
    s = x.shape[1]
    const = at(lambda bi, i: (0, 0))
    tile = pl.BlockSpec((1, ts, D_MODEL), at(lambda bi, i: (bi, i, 0)))
    single = pl.Buffered(1)
    in_specs = [
        tile,
        *_halo_specs(D_MODEL, s, ts, at),
        pl.BlockSpec((1, 1, 6 * D_MODEL), at(lambda bi, i: (bi, 0, 0))),
        pl.BlockSpec((1, D_MODEL), const),
        pl.BlockSpec((D_MODEL, 2 * D_FF), const, pipeline_mode=single),
        pl.BlockSpec((3, D_FF), const),
        pl.BlockSpec((D_FF, D_MODEL), const, pipeline_mode=single),
    ]
    scratch = [pltpu.VMEM((ts + 2 * HALO, D_MODEL), bf16),
               pltpu.VMEM((ts + 2 * HALO, ch), f32),
               pltpu.VMEM((ts, D_FF), bf16)]
    args = (x, x, x, modl, g2, w_up_b, ffn_dw, w_down_b)
    return args, in_specs, tile, scratch


def _ffn(x, ffn_args, ts=512, ch=256):
    b, s, _ = x.shape
    args, in_specs, out_spec, scratch = _ffn_io(x, *ffn_args, ts, ch, lambda f: f)
    return pl.pallas_call(
        functools.partial(_ffn_kernel, ts=ts, ch=ch),
        out_shape=jax.ShapeDtypeStruct(x.shape, f32),
        grid=(b, s // ts),
        in_specs=in_specs,
        out_specs=out_spec,
        scratch_shapes=scratch,
        compiler_params=_params(("parallel", "parallel")),
        name="ffn",
    )(*args)


def _advance(gen, value):
    try:
        gen.send(value)
    except StopIteration:
        pass


def _ffn_mix_kernel(*refs, nt_f, nt_x, ts, rc, ch):
    n_in = N_FFN_IN + N_MIX_IN
    f_in, x_in = refs[:N_FFN_IN], refs[N_FFN_IN:n_in]
    y_f, y_x = refs[n_in:n_in + 2]
    f_scr, x_scr = refs[n_in + 2:n_in + 2 + N_SCRATCH], refs[n_in + 2 + N_SCRATCH:]
    j = pl.program_id(0)
    ffn = _ffn_body(j % nt_f, nt_f - 1, *f_in, y_f, *f_scr, ts=ts, ch=ch)
    next(ffn)
    for tie in _mix_body(j % nt_x, nt_x - 1, *x_in, y_x, *x_scr, ts=ts, rc=rc, unroll=True):
        _advance(ffn, tie)
    for _ in ffn:
        pass


def _ffn_mix(x_f, ffn_args, x_x, mix_args, ts=256, rc=32, ch=256):
    nt_f, nt_x = x_f.shape[1] // ts, x_x.shape[1] // ts
    steps = x_f.shape[0] * nt_f
    assert steps == x_x.shape[0] * nt_x
    flat = lambda nt: (lambda f: (lambda j: f(j // nt, j % nt)))
    f_args, f_specs, f_out, f_scr = _ffn_io(x_f, *ffn_args, ts, ch, flat(nt_f))
    x_args, x_specs, x_out, x_scr = _mix_io(x_x, *mix_args, ts, flat(nt_x))
    return pl.pallas_call(
        functools.partial(_ffn_mix_kernel, nt_f=nt_f, nt_x=nt_x, ts=ts, rc=rc, ch=ch),
        out_shape=[jax.ShapeDtypeStruct(x_f.shape, f32), jax.ShapeDtypeStruct(x_x.shape, f32)],
        grid=(steps,),
        in_specs=f_specs + x_specs,
        out_specs=[f_out, x_out],
        scratch_shapes=f_scr + x_scr,
        compiler_params=_params(("parallel",)),
        name="ffn_mix",
    )(*f_args, *x_args)


def _pre_mix(x, modl, tables, p, l):
    outs = _inproj(x, modl, p["norm1_g"][l][None], p["w_in"][l], p["e_mat"],
                   p["q_g"][l][None], p["k_g"][l][None])
    ol = []
    for pi, (_, d) in enumerate(PATTERNS):
        q, k, v = outs[3 * pi:3 * pi + 3]
        if d == 1:
            q, k, v = q[:, None], k[:, None], v[:, None]
        ol += _attention(q, k, v, tables[pi], d)
    return (modl, ol, outs[9],
            jnp.broadcast_to(p["conv_dw_w"][l][:, None, :], (CONV_KERNEL, SUBLANES, D_CONV)),
            jnp.broadcast_to(p["conv_dw_b"][l][None], (SUBLANES, D_CONV)),
            p["conv_ln_g"][l][None], p["conv_ln_b"][l][None], p["w_out"][l])


def _ffn_operands(modl, p, l):
    return (modl, p["norm2_g"][l][None], p["w_up"][l], p["ffn_dw_w"][l], p["w_down"][l])


def _trunks(x_a, x_b, mod_a, mod_b, tables_a, tables_b, p):
    depth = mod_a.shape[0]
    ml = lambda mod, l: mod[l][:, None, :]
    x_a = _mix(x_a, _pre_mix(x_a, ml(mod_a, 0), tables_a, p, 0))
    mix_b = _pre_mix(x_b, ml(mod_b, 0), tables_b, p, 0)
    for l in range(depth):
        x_a, x_b = _ffn_mix(x_a, _ffn_operands(ml(mod_a, l), p, l), x_b, mix_b)
        if l + 1 < depth:
            mix_a = _pre_mix(x_a, ml(mod_a, l + 1), tables_a, p, l + 1)
            x_b, x_a = _ffn_mix(x_b, _ffn_operands(ml(mod_b, l), p, l), x_a, mix_a)
            mix_b = _pre_mix(x_b, ml(mod_b, l + 1), tables_b, p, l + 1)
        else:
            x_b = _ffn(x_b, _ffn_operands(ml(mod_b, l), p, l))
    return x_a, x_b


def kernel(x_prompt, x_sample, c_prompt, c_sample, rel_bias, norm1_g, norm2_g, w_ada, b_ada, w_in, q_norm_g, k_norm_g, conv_dw_w, conv_dw_b, conv_ln_g, conv_ln_b, w_out, w_up, ffn_dw_w, w_down):
    nb = c_prompt.shape[0]
    mod = _modulation(jnp.concatenate([c_prompt, c_sample], axis=0), w_ada, b_ada)
    head_id = np.arange(D_ATT) // HEAD_DIM
    e_mat = jnp.asarray((head_id[:, None] == head_id[None, :]) / HEAD_DIM, dtype=bf16)
    p = dict(
        norm1_g=norm1_g, norm2_g=norm2_g,
        w_in=w_in.astype(bf16), w_out=w_out.astype(bf16),
        w_up=w_up.astype(bf16), w_down=w_down.astype(bf16),
        q_g=jnp.tile(q_norm_g, (1, N_HEADS)) * (ATT_SCALE * LOG2E), k_g=jnp.tile(k_norm_g, (1, N_HEADS)),
        conv_dw_w=conv_dw_w, conv_dw_b=conv_dw_b, conv_ln_g=conv_ln_g, conv_ln_b=conv_ln_b,
        ffn_dw_w=ffn_dw_w, e_mat=e_mat,
    )
    tables = [[_bias_table(rel_bias, d, min(2 * QT, x.shape[1] // d)) for _, d in PATTERNS]
              for x in (x_prompt, x_sample)]
    return _trunks(x_prompt, x_sample, mod[:, :nb], mod[:, nb:], tables[0], tables[1], p)
```

```python
import functools
import math

import numpy as np
import jax
import jax.numpy as jnp
from jax import lax
from jax.experimental import pallas as pl
from jax.experimental.pallas import tpu as pltpu

D_MODEL = 1024
HEAD_DIM = 64
N_HEADS = 8
D_ATT = N_HEADS * HEAD_DIM
D_CONV = D_MODEL - D_ATT
D_IN = 3 * D_ATT + 2 * D_CONV
CONV_KERNEL = 31
D_FF = 2816
PATTERNS = ((128, 1), (512, 4), (2048, 16))
SIDE = 64
N_BUCKETS = 32
REL_MAX_DIST = 1024
EPS = 1e-6
NEG = -1e30
ATT_SCALE = 1.0 / math.sqrt(HEAD_DIM)
LOG2E = 1.0 / math.log(2.0)
LN2 = math.log(2.0)

LANES = 128
SUBLANES = 8
QT = 128
STEP_TOKENS = 2048
HALO = 16
VMEM_LIMIT = 56 * 1024 * 1024

f32 = jnp.float32
bf16 = jnp.bfloat16


def _dot(a, b):
    return jnp.dot(a, b, preferred_element_type=f32)


def _sigmoid(x):
    return 0.5 * jnp.tanh(0.5 * x) + 0.5


def _split_bf16(x):
    hi = x.astype(bf16)
    lo = (x - hi.astype(f32)).astype(bf16)
    return hi, lo


def _params(sem):
    return pltpu.CompilerParams(dimension_semantics=sem, vmem_limit_bytes=VMEM_LIMIT)


def _mod_kernel(c_ref, w_ref, b_ref, o_ref):
    c = c_ref[...]
    sc_hi, sc_lo = _split_bf16(c * _sigmoid(c))
    w_hi, w_lo = _split_bf16(w_ref[0])
    acc = _dot(sc_hi, w_hi) + _dot(sc_hi, w_lo) + _dot(sc_lo, w_hi)
    o_ref[0] = acc + b_ref[0]


def _modulation(c_all, w_ada, b_ada):
    depth, _, n_out = w_ada.shape
    rows = c_all.shape[0]
    nb = 1536
    return pl.pallas_call(
        _mod_kernel,
        out_shape=jax.ShapeDtypeStruct((depth, rows, n_out), f32),
        grid=(depth, n_out // nb),
        in_specs=[
            pl.BlockSpec((rows, D_MODEL), lambda l, j: (0, 0)),
            pl.BlockSpec((1, D_MODEL, nb), lambda l, j: (l, 0, j)),
            pl.BlockSpec((1, 1, nb), lambda l, j: (l, 0, j)),
        ],
        out_specs=pl.BlockSpec((1, rows, nb), lambda l, j: (l, 0, j)),
        compiler_params=_params(("arbitrary", "arbitrary")),
        name="adaln_mod",
    )(c_all, w_ada, b_ada.reshape(depth, 1, n_out))


def _inproj_kernel(x_ref, mod_ref, g1_ref, w_ref, e_ref, qg_ref, kg_ref,
                   q1_ref, k1_ref, v1_ref, q4_ref, k4_ref, v4_ref, q16_ref, k16_ref, v16_ref,
                   u_ref, scr, scr4, *, ts):
    x = x_ref[0]
    ms = jnp.mean(x * x, axis=-1, keepdims=True)
    shift = mod_ref[0, :, 0:D_MODEL]
    scale = mod_ref[0, :, D_MODEL:2 * D_MODEL]
    h = (x * lax.rsqrt(ms + EPS)) * (g1_ref[...] * (1.0 + scale)) + shift
    proj = _dot(h.astype(bf16), w_ref[...])

    e = e_ref[...]

    def head_norm(t, g):
        msh = _dot((t * t).astype(bf16), e)
        return t * lax.rsqrt(msh + EPS) * g

    q = head_norm(proj[:, 0:D_ATT], qg_ref[...])
    k = head_norm(proj[:, D_ATT:2 * D_ATT], kg_ref[...])
    v = proj[:, 2 * D_ATT:3 * D_ATT]
    cv = proj[:, 3 * D_ATT:3 * D_ATT + D_CONV]
    cg = proj[:, 3 * D_ATT + D_CONV:]
    u_ref[0] = cv * _sigmoid(cg)

    q1_ref[0] = q.astype(bf16)
    k1_ref[0] = k.astype(bf16)
    v1_ref[0] = v.astype(bf16)
    n_slab = D_ATT // LANES
    for a, t in enumerate((q, k, v)):
        for s in range(n_slab):
            scr[a * n_slab + s] = t[:, s * LANES:(s + 1) * LANES]
    n4, n16 = ts // 4, ts // 16
    for a, (ref4, ref16) in enumerate(((q4_ref, q16_ref), (k4_ref, k16_ref), (v4_ref, v16_ref))):
        for s in range(n_slab):
            cols = slice(s * LANES, (s + 1) * LANES)
            for r4 in range(4):
                part = scr[a * n_slab + s, pl.ds(r4, n4, stride=4), :]
                ref4[0, r4, :, cols] = part.astype(bf16)
                scr4[a * n_slab + s, r4 * n4:(r4 + 1) * n4, :] = part
            for r16 in range(16):
                part = scr4[a * n_slab + s, pl.ds((r16 % 4) * n4 + r16 // 4, n16, stride=4), :]
                ref16[0, r16, :, cols] = part.astype(bf16)


def _inproj(x, modl, g1, w_in_b, e_mat, qg, kg, ts=512):
    b, s, _ = x.shape
    nt = s // ts
    nat = jax.ShapeDtypeStruct((b, s, D_ATT), bf16)
    out_shape = [nat, nat, nat]
    out_specs = [pl.BlockSpec((1, ts, D_ATT), lambda bi, i: (bi, i, 0))] * 3
    for d in (4, 16):
        out_shape += [jax.ShapeDtypeStruct((b, d, s // d, D_ATT), bf16)] * 3
        out_specs += [pl.BlockSpec((1, d, ts // d, D_ATT), lambda bi, i: (bi, 0, i, 0))] * 3
    out_shape.append(jax.ShapeDtypeStruct((b, s, D_CONV), f32))
    out_specs.append(pl.BlockSpec((1, ts, D_CONV), lambda bi, i: (bi, i, 0)))
    const = lambda bi, i: (0, 0)
    return pl.pallas_call(
        functools.partial(_inproj_kernel, ts=ts),
        out_shape=out_shape,
        grid=(b, nt),
        in_specs=[
            pl.BlockSpec((1, ts, D_MODEL), lambda bi, i: (bi, i, 0)),
            pl.BlockSpec((1, 1, 6 * D_MODEL), lambda bi, i: (bi, 0, 0)),
            pl.BlockSpec((1, D_MODEL), const),
            pl.BlockSpec((D_MODEL, D_IN), const),
            pl.BlockSpec((D_ATT, D_ATT), const),
            pl.BlockSpec((1, D_ATT), const),
            pl.BlockSpec((1, D_ATT), const),
        ],
        out_specs=out_specs,
        scratch_shapes=[pltpu.VMEM((3 * D_ATT // LANES, ts, LANES), f32)] * 2,
        compiler_params=_params(("parallel", "parallel")),
        name="inproj",
    )(x, modl, g1, w_in_b, e_mat, qg, kg)


def _attn_kernel(q_ref, k_ref, v_ref, tbl_ref, o_ref, l_ref, p_a, p_b, m_a, m_b,
                 *, d, g_tiles, seq, win):
    i = pl.program_id(1)
    n_tiles = seq // QT
    n_it = d * g_tiles
    lane = lax.broadcasted_iota(jnp.int32, (1, LANES), 1)
    lo_half = lane < HEAD_DIM
    halves = (lo_half, jnp.logical_not(lo_half))
    n_pair = D_ATT // LANES

    def coords(it):
        r = it // g_tiles
        g = it % g_tiles
        t = i * g_tiles + g
        start = pl.multiple_of(jnp.clip(t * QT - SIDE, 0, seq - win), SIDE)
        var = jnp.where(t == 0, 0, jnp.where(t == n_tiles - 1, 2, 1))
        row0 = pl.multiple_of(g * QT, QT)
        return r, start, var, row0

    def scores(it, p_scr, m_scr):
        r, start, var, row0 = coords(it)
        for hp in range(n_pair):
            cols = slice(hp * LANES, (hp + 1) * LANES)
            qp = q_ref[0, r, pl.ds(row0, QT), cols]
            kp = k_ref[0, r, pl.ds(start, win), cols]
            mx = []
            for e in range(2):
                qm = jnp.where(halves[e], qp, jnp.zeros_like(qp))
                s = lax.dot_general(qm, kp, (((1,), (1,)), ((), ())), preferred_element_type=f32)
                s = s + tbl_ref[var, 2 * hp + e]
                m = jnp.max(s, axis=-1, keepdims=True)
                p_scr[2 * hp + e] = jnp.exp2(s - m).astype(bf16)
                mx.append(m)
            m_scr[hp] = jnp.where(lo_half, mx[0], mx[1])

    def values(it, p_scr, m_scr):
        r, start, var, row0 = coords(it)
        for hp in range(n_pair):
            cols = slice(hp * LANES, (hp + 1) * LANES)
            vp = v_ref[0, r, pl.ds(start, win), cols]
            pv = [_dot(p_scr[2 * hp + e], jnp.where(halves[e], vp, jnp.ones_like(vp)))
                  for e in range(2)]
            num = jnp.where(lo_half, pv[0], pv[1])
            den = pltpu.roll(jnp.where(lo_half, pv[1], pv[0]), HEAD_DIM, axis=1)
            if d == 1:
                rows = pl.ds(row0, QT)
            else:
                rows = pl.ds(row0 * d + r, QT, stride=d)
            o_ref[0, hp, rows, :] = num / den
            l_ref[0, hp, rows, :] = m_scr[hp] * LN2 + jnp.log(den)

    scores(0, p_a, m_a)

    def body(j, carry):
        scores(2 * j + 1, p_b, m_b)
        values(2 * j, p_a, m_a)
        scores(2 * j + 2, p_a, m_a)
        values(2 * j + 1, p_b, m_b)
        return carry

    lax.fori_loop(0, n_it // 2 - 1, body, 0)
    scores(n_it - 1, p_b, m_b)
    values(n_it - 2, p_a, m_a)
    values(n_it - 1, p_b, m_b)


def _attention(qd, kd, vd, tbl, d):
    b, _, seq, _ = qd.shape
    s_nat = seq * d
    g_tiles = STEP_TOKENS // (QT * d)
    win = tbl.shape[-1]
    n_slab = D_ATT // LANES
    out = jax.ShapeDtypeStruct((b, n_slab, s_nat, LANES), f32)
    out_spec = pl.BlockSpec((1, n_slab, STEP_TOKENS, LANES), lambda bi, i: (bi, 0, i, 0))
    full = pl.BlockSpec((1, d, seq, D_ATT), lambda bi, i: (bi, 0, 0, 0))
    p_slot = pltpu.VMEM((N_HEADS, QT, win), bf16)
    m_slot = pltpu.VMEM((n_slab, QT, LANES), f32)
    return pl.pallas_call(
        functools.partial(_attn_kernel, d=d, g_tiles=g_tiles, seq=seq, win=win),
        out_shape=[out, out],
        grid=(b, s_nat // STEP_TOKENS),
        in_specs=[
            pl.BlockSpec((1, d, QT * g_tiles, D_ATT), lambda bi, i: (bi, 0, i, 0)),
            full,
            full,
            pl.BlockSpec(tbl.shape, lambda bi, i: (0, 0, 0, 0)),
        ],
        out_specs=[out_spec, out_spec],
        scratch_shapes=[p_slot, p_slot, m_slot, m_slot],
        compiler_params=_params(("parallel", "arbitrary")),
        name=f"attn_d{d}",
    )(qd, kd, vd, tbl)


def _t5_bucket(rel):
    n = -rel
    half = N_BUCKETS // 2
    ret = (n < 0).astype(np.int32) * half
    n = np.abs(n)
    max_exact = half // 2
    large = max_exact + (np.log(np.maximum(n, 1) / max_exact) / np.log(REL_MAX_DIST / max_exact)
                         * (half - max_exact)).astype(np.int32)
    large = np.minimum(large, half - 1)
    return (ret + np.where(n < max_exact, n, large)).astype(np.int32)


def _table_kernel(bias_ref, idx_ref, o_ref):
    h = pl.program_id(1)
    idx = idx_ref[0]
    t = jnp.full(idx.shape, NEG, f32)
    for bucket in range(N_BUCKETS):
        t = jnp.where(idx == bucket, bias_ref[bucket, h], t)
    o_ref[0, 0] = t * LOG2E


def _bias_table(rel_bias, d, win):
    t = np.arange(QT)[None, :, None]
    u = np.arange(win)[None, None, :]
    offset = np.array([0, SIDE, 2 * SIDE])[:, None, None]
    rel = u - t - offset
    idx = np.where(np.abs(rel) <= SIDE, _t5_bucket(np.clip(rel, -SIDE, SIDE) * d), -1)
    return pl.pallas_call(
        _table_kernel,
        out_shape=jax.ShapeDtypeStruct((3, N_HEADS, QT, win), f32),
        grid=(3, N_HEADS),
        in_specs=[
            pl.BlockSpec(memory_space=pltpu.SMEM),
            pl.BlockSpec((1, QT, win), lambda v, h: (v, 0, 0)),
        ],
        out_specs=pl.BlockSpec((1, 1, QT, win), lambda v, h: (v, h, 0, 0)),
        compiler_params=_params(("arbitrary", "arbitrary")),
        name=f"bias_table_d{d}",
    )(rel_bias, jnp.asarray(idx, jnp.int32))


def _mix_body(i, last, x_ref, mod_ref, o1_ref, l1_ref, o2_ref, l2_ref, o3_ref, l3_ref,
              u_ref, up_ref, un_ref, dww_ref, dwb_ref, lng_ref, lnb_ref, wout_ref,
              y_ref, uext, conv, cat, *, ts, rc):
    for s in range(D_ATT // LANES):
        la, lb, lc = l1_ref[0, s], l2_ref[0, s], l3_ref[0, s]
        m = jnp.maximum(jnp.maximum(la, lb), lc)
        ea, eb, ec = jnp.exp(la - m), jnp.exp(lb - m), jnp.exp(lc - m)
        att = (ea * o1_ref[0, s] + eb * o2_ref[0, s] + ec * o3_ref[0, s]) / (ea + eb + ec)
        cat[:, s * LANES:(s + 1) * LANES] = att.astype(bf16)

    uext[0, 0:HALO, :] = jnp.where(i > 0, up_ref[0], 0.0)
    uext[0, HALO:HALO + ts, :] = u_ref[0]
    uext[0, HALO + ts:2 * HALO + ts, :] = jnp.where(i < last, un_ref[0], 0.0)
    n_sh = ts + 2 * HALO - SUBLANES
    for sh in range(1, SUBLANES):
        uext[sh, 0:n_sh, :] = uext[0, sh:sh + n_sh, :]
    pad = HALO - CONV_KERNEL // 2

    def chunk(c, carry):
        base = pl.multiple_of(c * rc, rc)
        acc = [dwb_ref[...]] * (rc // SUBLANES)
        for k in range(CONV_KERNEL):
            off = k + pad
            w = dww_ref[k]
            row = base + (off // SUBLANES) * SUBLANES
            for g in range(rc // SUBLANES):
                acc[g] = acc[g] + w * uext[off % SUBLANES, pl.ds(row + g * SUBLANES, SUBLANES), :]
        conv[pl.ds(base, rc), :] = jnp.concatenate(acc, axis=0)
        return carry

    lax.fori_loop(0, ts // rc, chunk, 0)
    cv = conv[...]
    mu = jnp.mean(cv, axis=-1, keepdims=True)
    xc = cv - mu
    var = jnp.mean(xc * xc, axis=-1, keepdims=True)
    yn = xc * lax.rsqrt(var + EPS) * lng_ref[...] + lnb_ref[...]
    cat[:, D_ATT:] = (yn * _sigmoid(yn)).astype(bf16)
    y = _dot(cat[...], wout_ref[...])
    gate = mod_ref[0, :, 2 * D_MODEL:3 * D_MODEL]
    y_ref[0] = x_ref[0] + gate * y


def _mix_kernel(*refs, ts, rc):
    _mix_body(pl.program_id(1), pl.num_programs(1) - 1, *refs, ts=ts, rc=rc)


def _halo_specs(width, s, ts, at):
    hb = ts // HALO
    prev = pl.BlockSpec((1, HALO, width), at(lambda bi, i: (bi, jnp.maximum(i * hb - 1, 0), 0)))
    nxt = pl.BlockSpec((1, HALO, width),
                       at(lambda bi, i: (bi, jnp.minimum((i + 1) * hb, s // HALO - 1), 0)))
    return prev, nxt


def _mix_io(x, modl, ol, u, dww, dwb, lng, lnb, w_out_b, ts, at):
    s = x.shape[1]
    n_slab = D_ATT // LANES
    slab = pl.BlockSpec((1, n_slab, ts, LANES), at(lambda bi, i: (bi, 0, i, 0)))
    const = at(lambda bi, i: (0, 0))
    tile = lambda w: pl.BlockSpec((1, ts, w), at(lambda bi, i: (bi, i, 0)))
    in_specs = [
        tile(D_MODEL),
        pl.BlockSpec((1, 1, 6 * D_MODEL), at(lambda bi, i: (bi, 0, 0))),
        slab, slab, slab, slab, slab, slab,
        tile(D_CONV),
        *_halo_specs(D_CONV, s, ts, at),
        pl.BlockSpec((CONV_KERNEL, SUBLANES, D_CONV), at(lambda bi, i: (0, 0, 0))),
        pl.BlockSpec((SUBLANES, D_CONV), const),
        pl.BlockSpec((1, D_CONV), const),
        pl.BlockSpec((1, D_CONV), const),
        pl.BlockSpec((D_MODEL, D_MODEL), const, pipeline_mode=pl.Buffered(1)),
    ]
    scratch = [pltpu.VMEM((SUBLANES, ts + 2 * HALO, D_CONV), f32),
               pltpu.VMEM((ts, D_CONV), f32),
               pltpu.VMEM((ts, D_MODEL), bf16)]
    args = (x, modl, *ol, u, u, u, dww, dwb, lng, lnb, w_out_b)
    return args, in_specs, tile(D_MODEL), scratch


def _mix(x, mix_args, ts=512, rc=32):
    b, s, _ = x.shape
    args, in_specs, out_spec, scratch = _mix_io(x, *mix_args, ts, lambda f: f)
    return pl.pallas_call(
        functools.partial(_mix_kernel, ts=ts, rc=rc),
        out_shape=jax.ShapeDtypeStruct(x.shape, f32),
        grid=(b, s // ts),
        in_specs=in_specs,
        out_specs=out_spec,
        scratch_shapes=scratch,
        compiler_params=_params(("parallel", "parallel")),
        name="mix",
    )(*args)


def _ffn_body(i, last, x_ref, xp_ref, xn_ref, mod_ref, g2_ref, wup_ref, dw_ref, wdn_ref,
              y_ref, hext, gext, z, *, ts, ch):
    shift = mod_ref[0, :, 3 * D_MODEL:4 * D_MODEL]
    scale = mod_ref[0, :, 4 * D_MODEL:5 * D_MODEL]
    gate = mod_ref[0, :, 5 * D_MODEL:6 * D_MODEL]
    a = g2_ref[...] * (1.0 + scale)

    def norm(xx):
        ms = jnp.mean(xx * xx, axis=-1, keepdims=True)
        return (xx * lax.rsqrt(ms + EPS)) * a + shift

    x = x_ref[0]
    hext[0:HALO, :] = jnp.where(i > 0, norm(xp_ref[0]), 0.0).astype(bf16)
    hext[HALO:HALO + ts, :] = norm(x).astype(bf16)
    hext[HALO + ts:2 * HALO + ts, :] = jnp.where(i < last, norm(xn_ref[0]), 0.0).astype(bf16)

    c_gelu = math.sqrt(2.0 / math.pi)
    for j in range(D_FF // ch):
        c0 = j * ch
        gext[...] = _dot(hext[...], wup_ref[:, D_FF + c0:D_FF + c0 + ch])
        gc = (dw_ref[0:1, c0:c0 + ch] * gext[HALO - 1:HALO - 1 + ts, :]
              + dw_ref[1:2, c0:c0 + ch] * gext[HALO:HALO + ts, :]
              + dw_ref[2:3, c0:c0 + ch] * gext[HALO + 1:HALO + 1 + ts, :])
        up = _dot(hext[HALO:HALO + ts, :], wup_ref[:, c0:c0 + ch])
        cdf = 0.5 * (1.0 + jnp.tanh(c_gelu * (gc + 0.044715 * (gc * gc * gc))))
        z[:, c0:c0 + ch] = (up * (gc * cdf)).astype(bf16)
    y = _dot(z[...], wdn_ref[...])
    y_ref[0] = x + gate * y


def _ffn_kernel(*refs, ts, ch):
    _ffn_body(pl.program_id(1), pl.num_programs(1) - 1, *refs, ts=ts, ch=ch)


def _ffn_io(x, modl, g2, w_up_b, ffn_dw, w_down_b, ts, ch, at):
    s = x.shape[1]
    const = at(lambda bi, i: (0, 0))
    tile = pl.BlockSpec((1, ts, D_MODEL), at(lambda bi, i: (bi, i, 0)))
    single = pl.Buffered(1)
    in_specs = [
        tile,
        *_halo_specs(D_MODEL, s, ts, at),
        pl.BlockSpec((1, 1, 6 * D_MODEL), at(lambda bi, i: (bi, 0, 0))),
        pl.BlockSpec((1, D_MODEL), const),
        pl.BlockSpec((D_MODEL, 2 * D_FF), const, pipeline_mode=single),
        pl.BlockSpec((3, D_FF), const),
        pl.BlockSpec((D_FF, D_MODEL), const, pipeline_mode=single),
    ]
    scratch = [pltpu.VMEM((ts + 2 * HALO, D_MODEL), bf16),
               pltpu.VMEM((ts + 2 * HALO, ch), f32),
               pltpu.VMEM((ts, D_FF), bf16)]
    args = (x, x, x, modl, g2, w_up_b, ffn_dw, w_down_b)
    return args, in_specs, tile, scratch


def _ffn(x, ffn_args, ts=1024, ch=256):
    b, s, _ = x.shape
    args, in_specs, out_spec, scratch = _ffn_io(x, *ffn_args, ts, ch, lambda f: f)
    return pl.pallas_call(
        functools.partial(_ffn_kernel, ts=ts, ch=ch),
        out_shape=jax.ShapeDtypeStruct(x.shape, f32),
        grid=(b, s // ts),
        in_specs=in_specs,
        out_specs=out_spec,
        scratch_shapes=scratch,
        compiler_params=_params(("parallel", "parallel")),
        name="ffn",
    )(*args)


def _pre_mix(x, modl, tables, p, l):
    outs = _inproj(x, modl, p["norm1_g"][l][None], p["w_in"][l], p["e_mat"],
                   p["q_g"][l][None], p["k_g"][l][None])
    ol = []
    for pi, (_, d) in enumerate(PATTERNS):
        q, k, v = outs[3 * pi:3 * pi + 3]
        if d == 1:
            q, k, v = q[:, None], k[:, None], v[:, None]
        ol += _attention(q, k, v, tables[pi], d)
    return (modl, ol, outs[9],
            jnp.broadcast_to(p["conv_dw_w"][l][:, None, :], (CONV_KERNEL, SUBLANES, D_CONV)),
            jnp.broadcast_to(p["conv_dw_b"][l][None], (SUBLANES, D_CONV)),
            p["conv_ln_g"][l][None], p["conv_ln_b"][l][None], p["w_out"][l])


def _ffn_operands(modl, p, l):
    return (modl, p["norm2_g"][l][None], p["w_up"][l], p["ffn_dw_w"][l], p["w_down"][l])


def _trunk(x, mod, tables, p):
    for l in range(mod.shape[0]):
        modl = mod[l][:, None, :]
        x = _mix(x, _pre_mix(x, modl, tables, p, l))
        x = _ffn(x, _ffn_operands(modl, p, l))
    return x


def kernel(x_prompt, x_sample, c_prompt, c_sample, rel_bias, norm1_g, norm2_g, w_ada, b_ada, w_in, q_norm_g, k_norm_g, conv_dw_w, conv_dw_b, conv_ln_g, conv_ln_b, w_out, w_up, ffn_dw_w, w_down):
    nb = c_prompt.shape[0]
    mod = _modulation(jnp.concatenate([c_prompt, c_sample], axis=0), w_ada, b_ada)
    head_id = np.arange(D_ATT) // HEAD_DIM
    e_mat = jnp.asarray((head_id[:, None] == head_id[None, :]) / HEAD_DIM, dtype=bf16)
    p = dict(
        norm1_g=norm1_g, norm2_g=norm2_g,
        w_in=w_in.astype(bf16), w_out=w_out.astype(bf16),
        w_up=w_up.astype(bf16), w_down=w_down.astype(bf16),
        q_g=jnp.tile(q_norm_g, (1, N_HEADS)) * (ATT_SCALE * LOG2E), k_g=jnp.tile(k_norm_g, (1, N_HEADS)),
        conv_dw_w=conv_dw_w, conv_dw_b=conv_dw_b, conv_ln_g=conv_ln_g, conv_ln_b=conv_ln_b,
        ffn_dw_w=ffn_dw_w, e_mat=e_mat,
    )
    tables = [[_bias_table(rel_bias, d, min(2 * QT, x.shape[1] // d)) for _, d in PATTERNS]
              for x in (x_prompt, x_sample)]
    return (_trunk(x_prompt, mod[:, :nb], tables[0], p), _trunk(x_sample, mod[:, nb:], tables[1], p))
```

```python
import functools
import math

import numpy as np
import jax
import jax.numpy as jnp
from jax import lax
from jax.experimental import pallas as pl
from jax.experimental.pallas import tpu as pltpu

D_MODEL = 1024
HEAD_DIM = 64
N_HEADS = 8
D_ATT = N_HEADS * HEAD_DIM
D_CONV = D_MODEL - D_ATT
D_IN = 3 * D_ATT + 2 * D_CONV
CONV_KERNEL = 31
D_FF = 2816
PATTERNS = ((128, 1), (512, 4), (2048, 16))
SIDE = 64
N_BUCKETS = 32
REL_MAX_DIST = 1024
EPS = 1e-6
NEG = -1e30
ATT_SCALE = 1.0 / math.sqrt(HEAD_DIM)
LOG2E = 1.0 / math.log(2.0)
LN2 = math.log(2.0)

LANES = 128
SUBLANES = 8
QT = 128
STEP_TOKENS = 2048
HALO = 16
VMEM_LIMIT = 56 * 1024 * 1024

f32 = jnp.float32
bf16 = jnp.bfloat16


def _dot(a, b):
    return jnp.dot(a, b, preferred_element_type=f32)


def _sigmoid(x):
    return 0.5 * jnp.tanh(0.5 * x) + 0.5


def _split_bf16(x):
    hi = x.astype(bf16)
    lo = (x - hi.astype(f32)).astype(bf16)
    return hi, lo


def _params(sem):
    return pltpu.CompilerParams(dimension_semantics=sem, vmem_limit_bytes=VMEM_LIMIT)


def _mod_kernel(c_ref, w_ref, b_ref, o_ref):
    c = c_ref[...]
    sc_hi, sc_lo = _split_bf16(c * _sigmoid(c))
    w_hi, w_lo = _split_bf16(w_ref[0])
    acc = _dot(sc_hi, w_hi) + _dot(sc_hi, w_lo) + _dot(sc_lo, w_hi)
    o_ref[0] = acc + b_ref[0]


def _modulation(c_all, w_ada, b_ada):
    depth, _, n_out = w_ada.shape
    rows = c_all.shape[0]
    nb = 1536
    return pl.pallas_call(
        _mod_kernel,
        out_shape=jax.ShapeDtypeStruct((depth, rows, n_out), f32),
        grid=(depth, n_out // nb),
        in_specs=[
            pl.BlockSpec((rows, D_MODEL), lambda l, j: (0, 0)),
            pl.BlockSpec((1, D_MODEL, nb), lambda l, j: (l, 0, j)),
            pl.BlockSpec((1, 1, nb), lambda l, j: (l, 0, j)),
        ],
        out_specs=pl.BlockSpec((1, rows, nb), lambda l, j: (l, 0, j)),
        compiler_params=_params(("arbitrary", "arbitrary")),
        name="adaln_mod",
    )(c_all, w_ada, b_ada.reshape(depth, 1, n_out))


def _inproj_kernel(x_ref, mod_ref, g1_ref, w_ref, e_ref, qg_ref, kg_ref,
                   qkv1_ref, qkv4_ref, qkv16_ref, u_ref, scr, scr4, *, ts):
    x = x_ref[0]
    ms = jnp.mean(x * x, axis=-1, keepdims=True)
    shift = mod_ref[0, :, 0:D_MODEL]
    scale = mod_ref[0, :, D_MODEL:2 * D_MODEL]
    h = (x * lax.rsqrt(ms + EPS)) * (g1_ref[...] * (1.0 + scale)) + shift
    proj = _dot(h.astype(bf16), w_ref[...])

    e = e_ref[...]

    def head_norm(t, g):
        msh = _dot((t * t).astype(bf16), e)
        return t * lax.rsqrt(msh + EPS) * g

    q = head_norm(proj[:, 0:D_ATT], qg_ref[...])
    k = head_norm(proj[:, D_ATT:2 * D_ATT], kg_ref[...])
    v = proj[:, 2 * D_ATT:3 * D_ATT]
    cv = proj[:, 3 * D_ATT:3 * D_ATT + D_CONV]
    cg = proj[:, 3 * D_ATT + D_CONV:]
    u_ref[0] = cv * _sigmoid(cg)

    n_slab = D_ATT // LANES
    for a, t in enumerate((q, k, v)):
        qkv1_ref[0, :, a * D_ATT:(a + 1) * D_ATT] = t.astype(bf16)
        for s in range(n_slab):
            scr[a * n_slab + s] = t[:, s * LANES:(s + 1) * LANES]
    n4, n16 = ts // 4, ts // 16
    for a in range(3):
        for s in range(n_slab):
            cols = slice(a * D_ATT + s * LANES, a * D_ATT + (s + 1) * LANES)
            for r4 in range(4):
                part = scr[a * n_slab + s, pl.ds(r4, n4, stride=4), :]
                qkv4_ref[0, r4, :, cols] = part.astype(bf16)
                scr4[a * n_slab + s, r4 * n4:(r4 + 1) * n4, :] = part
            for r16 in range(16):
                part = scr4[a * n_slab + s, pl.ds((r16 % 4) * n4 + r16 // 4, n16, stride=4), :]
                qkv16_ref[0, r16, :, cols] = part.astype(bf16)


def _inproj(x, modl, g1, w_in_b, l, e_mat, qg, kg, ts=512):
    b, s, _ = x.shape
    nt = s // ts
    out_shape = [jax.ShapeDtypeStruct((b, s, 3 * D_ATT), bf16)]
    out_specs = [pl.BlockSpec((1, ts, 3 * D_ATT), lambda bi, i: (bi, i, 0))]
    for d in (4, 16):
        out_shape.append(jax.ShapeDtypeStruct((b, d, s // d, 3 * D_ATT), bf16))
        out_specs.append(pl.BlockSpec((1, d, ts // d, 3 * D_ATT), lambda bi, i: (bi, 0, i, 0)))
    out_shape.append(jax.ShapeDtypeStruct((b, s, D_CONV), f32))
    out_specs.append(pl.BlockSpec((1, ts, D_CONV), lambda bi, i: (bi, i, 0)))
    const = lambda bi, i: (0, 0)
    return pl.pallas_call(
        functools.partial(_inproj_kernel, ts=ts),
        out_shape=out_shape,
        grid=(b, nt),
        in_specs=[
            pl.BlockSpec((1, ts, D_MODEL), lambda bi, i: (bi, i, 0)),
            pl.BlockSpec((1, 1, 6 * D_MODEL), lambda bi, i: (bi, 0, 0)),
            pl.BlockSpec((1, D_MODEL), const),
            pl.BlockSpec((None, D_MODEL, D_IN), lambda bi, i: (l, 0, 0)),
            pl.BlockSpec((D_ATT, D_ATT), const),
            pl.BlockSpec((1, D_ATT), const),
            pl.BlockSpec((1, D_ATT), const),
        ],
        out_specs=out_specs,
        scratch_shapes=[pltpu.VMEM((3 * D_ATT // LANES, ts, LANES), f32)] * 2,
        compiler_params=_params(("parallel", "parallel")),
        name="inproj",
    )(x, modl, g1, w_in_b, e_mat, qg, kg)


def _attn_kernel(q_ref, k_ref, v_ref, tbl_ref, o_ref, l_ref, p_a, p_b, m_a, m_b,
                 *, d, g_tiles, seq, win):
    i = pl.program_id(1)
    n_tiles = seq // QT
    n_it = d * g_tiles
    lane = lax.broadcasted_iota(jnp.int32, (1, LANES), 1)
    lo_half = lane < HEAD_DIM
    halves = (lo_half, jnp.logical_not(lo_half))
    n_pair = D_ATT // LANES

    def coords(it):
        r = it // g_tiles
        g = it % g_tiles
        t = i * g_tiles + g
        start = pl.multiple_of(jnp.clip(t * QT - SIDE, 0, seq - win), SIDE)
        var = jnp.where(t == 0, 0, jnp.where(t == n_tiles - 1, 2, 1))
        row0 = pl.multiple_of(g * QT, QT)
        return r, start, var, row0

    def scores(it, p_scr, m_scr):
        r, start, var, row0 = coords(it)
        for hp in range(n_pair):
            cols = slice(hp * LANES, (hp + 1) * LANES)
            qp = q_ref[0, r, pl.ds(row0, QT), cols]
            kp = k_ref[0, r, pl.ds(start, win), cols]
            mx = []
            for e in range(2):
                qm = jnp.where(halves[e], qp, jnp.zeros_like(qp))
                s = lax.dot_general(qm, kp, (((1,), (1,)), ((), ())), preferred_element_type=f32)
                s = s + tbl_ref[var, 2 * hp + e]
                m = jnp.max(s, axis=-1, keepdims=True)
                p_scr[2 * hp + e] = jnp.exp2(s - m).astype(bf16)
                mx.append(m)
            m_scr[hp] = jnp.where(lo_half, mx[0], mx[1])

    def values(it, p_scr, m_scr):
        r, start, var, row0 = coords(it)
        for hp in range(n_pair):
            cols = slice(hp * LANES, (hp + 1) * LANES)
            vp = v_ref[0, r, pl.ds(start, win), cols]
            pv = [_dot(p_scr[2 * hp + e], jnp.where(halves[e], vp, jnp.ones_like(vp)))
                  for e in range(2)]
            num = jnp.where(lo_half, pv[0], pv[1])
            den = pltpu.roll(jnp.where(lo_half, pv[1], pv[0]), HEAD_DIM, axis=1)
            if d == 1:
                rows = pl.ds(row0, QT)
            else:
                rows = pl.ds(row0 * d + r, QT, stride=d)
            o_ref[0, hp, rows, :] = num / den
            l_ref[0, hp, rows, :] = m_scr[hp] * LN2 + jnp.log(den)

    scores(0, p_a, m_a)

    def body(j, carry):
        scores(2 * j + 1, p_b, m_b)
        values(2 * j, p_a, m_a)
        scores(2 * j + 2, p_a, m_a)
        values(2 * j + 1, p_b, m_b)
        return carry

    lax.fori_loop(0, n_it // 2 - 1, body, 0)
    scores(n_it - 1, p_b, m_b)
    values(n_it - 2, p_a, m_a)
    values(n_it - 1, p_b, m_b)


def _attention(qkv, tbl, d):
    b, _, seq, _ = qkv.shape
    s_nat = seq * d
    g_tiles = STEP_TOKENS // (QT * d)
    win = tbl.shape[-1]
    n_slab = D_ATT // LANES
    out = jax.ShapeDtypeStruct((b, n_slab, s_nat, LANES), f32)
    out_spec = pl.BlockSpec((1, n_slab, STEP_TOKENS, LANES), lambda bi, i: (bi, 0, i, 0))
    full = lambda blk: pl.BlockSpec((1, d, seq, D_ATT), lambda bi, i: (bi, 0, 0, blk))
    p_slot = pltpu.VMEM((N_HEADS, QT, win), bf16)
    m_slot = pltpu.VMEM((n_slab, QT, LANES), f32)
    return pl.pallas_call(
        functools.partial(_attn_kernel, d=d, g_tiles=g_tiles, seq=seq, win=win),
        out_shape=[out, out],
        grid=(b, s_nat // STEP_TOKENS),
        in_specs=[
            pl.BlockSpec((1, d, QT * g_tiles, D_ATT), lambda bi, i: (bi, 0, i, 0)),
            full(1),
            full(2),
            pl.BlockSpec(tbl.shape, lambda bi, i: (0, 0, 0, 0)),
        ],
        out_specs=[out_spec, out_spec],
        scratch_shapes=[p_slot, p_slot, m_slot, m_slot],
        compiler_params=_params(("parallel", "arbitrary")),
        name=f"attn_d{d}",
    )(qkv, qkv, qkv, tbl)


def _t5_bucket(rel):
    n = -rel
    half = N_BUCKETS // 2
    ret = (n < 0).astype(np.int32) * half
    n = np.abs(n)
    max_exact = half // 2
    large = max_exact + (np.log(np.maximum(n, 1) / max_exact) / np.log(REL_MAX_DIST / max_exact)
                         * (half - max_exact)).astype(np.int32)
    large = np.minimum(large, half - 1)
    return (ret + np.where(n < max_exact, n, large)).astype(np.int32)


def _table_kernel(bias_ref, idx_ref, o_ref):
    h = pl.program_id(1)
    idx = idx_ref[0]
    t = jnp.full(idx.shape, NEG, f32)
    for bucket in range(N_BUCKETS):
        t = jnp.where(idx == bucket, bias_ref[bucket, h], t)
    o_ref[0, 0] = t * LOG2E


def _bias_table(rel_bias, d, win):
    t = np.arange(QT)[None, :, None]
    u = np.arange(win)[None, None, :]
    offset = np.array([0, SIDE, 2 * SIDE])[:, None, None]
    rel = u - t - offset
    idx = np.where(np.abs(rel) <= SIDE, _t5_bucket(np.clip(rel, -SIDE, SIDE) * d), -1)
    return pl.pallas_call(
        _table_kernel,
        out_shape=jax.ShapeDtypeStruct((3, N_HEADS, QT, win), f32),
        grid=(3, N_HEADS),
        in_specs=[
            pl.BlockSpec(memory_space=pltpu.SMEM),
            pl.BlockSpec((1, QT, win), lambda v, h: (v, 0, 0)),
        ],
        out_specs=pl.BlockSpec((1, 1, QT, win), lambda v, h: (v, h, 0, 0)),
        compiler_params=_params(("arbitrary", "arbitrary")),
        name=f"bias_table_d{d}",
    )(rel_bias, jnp.asarray(idx, jnp.int32))


def _mix_body(i, last, x_ref, mod_ref, o1_ref, l1_ref, o2_ref, l2_ref, o3_ref, l3_ref,
              u_ref, up_ref, un_ref, dww_ref, dwb_ref, lng_ref, lnb_ref, wout_ref,
              y_ref, uext, conv, cat, *, ts, rc):
    for s in range(D_ATT // LANES):
        la, lb, lc = l1_ref[0, s], l2_ref[0, s], l3_ref[0, s]
        m = jnp.maximum(jnp.maximum(la, lb), lc)
        ea, eb, ec = jnp.exp(la - m), jnp.exp(lb - m), jnp.exp(lc - m)
        att = (ea * o1_ref[0, s] + eb * o2_ref[0, s] + ec * o3_ref[0, s]) / (ea + eb + ec)
        cat[:, s * LANES:(s + 1) * LANES] = att.astype(bf16)

    uext[0, 0:HALO, :] = jnp.where(i > 0, up_ref[0], 0.0)
    uext[0, HALO:HALO + ts, :] = u_ref[0]
    uext[0, HALO + ts:2 * HALO + ts, :] = jnp.where(i < last, un_ref[0], 0.0)
    n_sh = ts + 2 * HALO - SUBLANES
    for sh in range(1, SUBLANES):
        uext[sh, 0:n_sh, :] = uext[0, sh:sh + n_sh, :]
    pad = HALO - CONV_KERNEL // 2

    def chunk(c, carry):
        base = pl.multiple_of(c * rc, rc)
        acc = [dwb_ref[...]] * (rc // SUBLANES)
        for k in range(CONV_KERNEL):
            off = k + pad
            w = dww_ref[k]
            row = base + (off // SUBLANES) * SUBLANES
            for g in range(rc // SUBLANES):
                acc[g] = acc[g] + w * uext[off % SUBLANES, pl.ds(row + g * SUBLANES, SUBLANES), :]
        conv[pl.ds(base, rc), :] = jnp.concatenate(acc, axis=0)
        return carry

    lax.fori_loop(0, ts // rc, chunk, 0)
    cv = conv[...]
    mu = jnp.mean(cv, axis=-1, keepdims=True)
    xc = cv - mu
    var = jnp.mean(xc * xc, axis=-1, keepdims=True)
    yn = xc * lax.rsqrt(var + EPS) * lng_ref[...] + lnb_ref[...]
    cat[:, D_ATT:] = (yn * _sigmoid(yn)).astype(bf16)
    y = _dot(cat[...], wout_ref[...])
    gate = mod_ref[0, :, 2 * D_MODEL:3 * D_MODEL]
    y_ref[0] = x_ref[0] + gate * y


def _mix_kernel(*refs, ts, rc):
    _mix_body(pl.program_id(1), pl.num_programs(1) - 1, *refs, ts=ts, rc=rc)


def _halo_specs(width, s, ts, at):
    hb = ts // HALO
    prev = pl.BlockSpec((1, HALO, width), at(lambda bi, i: (bi, jnp.maximum(i * hb - 1, 0), 0)))
    nxt = pl.BlockSpec((1, HALO, width),
                       at(lambda bi, i: (bi, jnp.minimum((i + 1) * hb, s // HALO - 1), 0)))
    return prev, nxt


def _mix_io(x, modl, ol, u, dww, dwb, lng, lnb, w_out_b, l, ts, at):
    s = x.shape[1]
    n_slab = D_ATT // LANES
    slab = pl.BlockSpec((1, n_slab, ts, LANES), at(lambda bi, i: (bi, 0, i, 0)))
    const = at(lambda bi, i: (0, 0))
    tile = lambda w: pl.BlockSpec((1, ts, w), at(lambda bi, i: (bi, i, 0)))
    in_specs = [
        tile(D_MODEL),
        pl.BlockSpec((1, 1, 6 * D_MODEL), at(lambda bi, i: (bi, 0, 0))),
        slab, slab, slab, slab, slab, slab,
        tile(D_CONV),
        *_halo_specs(D_CONV, s, ts, at),
        pl.BlockSpec((CONV_KERNEL, SUBLANES, D_CONV), at(lambda bi, i: (0, 0, 0))),
        pl.BlockSpec((SUBLANES, D_CONV), const),
        pl.BlockSpec((1, D_CONV), const),
        pl.BlockSpec((1, D_CONV), const),
        pl.BlockSpec((None, D_MODEL, D_MODEL), at(lambda bi, i: (l, 0, 0)),
                     pipeline_mode=pl.Buffered(1)),
    ]
    scratch = [pltpu.VMEM((SUBLANES, ts + 2 * HALO, D_CONV), f32),
               pltpu.VMEM((ts, D_CONV), f32),
               pltpu.VMEM((ts, D_MODEL), bf16)]
    args = (x, modl, *ol, u, u, u, dww, dwb, lng, lnb, w_out_b)
    return args, in_specs, tile(D_MODEL), scratch


def _mix(x, mix_args, ts=512, rc=32):
    b, s, _ = x.shape
    args, in_specs, out_spec, scratch = _mix_io(x, *mix_args, ts, lambda f: f)
    return pl.pallas_call(
        functools.partial(_mix_kernel, ts=ts, rc=rc),
        out_shape=jax.ShapeDtypeStruct(x.shape, f32),
        grid=(b, s // ts),
        in_specs=in_specs,
        out_specs=out_spec,
        scratch_shapes=scratch,
        compiler_params=_params(("parallel", "parallel")),
        name="mix",
    )(*args)


def _ffn_body(i, last, x_ref, xp_ref, xn_ref, mod_ref, g2_ref, wup_ref, dw_ref, wdn_ref,
              y_ref, hext, gext, z, *, ts, ch):
    shift = mod_ref[0, :, 3 * D_MODEL:4 * D_MODEL]
    scale = mod_ref[0, :, 4 * D_MODEL:5 * D_MODEL]
    gate = mod_ref[0, :, 5 * D_MODEL:6 * D_MODEL]
    a = g2_ref[...] * (1.0 + scale)

    def norm(xx):
        ms = jnp.mean(xx * xx, axis=-1, keepdims=True)
        return (xx * lax.rsqrt(ms + EPS)) * a + shift

    x = x_ref[0]
    hext[0:HALO, :] = jnp.where(i > 0, norm(xp_ref[0]), 0.0).astype(bf16)
    hext[HALO:HALO + ts, :] = norm(x).astype(bf16)
    hext[HALO + ts:2 * HALO + ts, :] = jnp.where(i < last, norm(xn_ref[0]), 0.0).astype(bf16)

    c_gelu = math.sqrt(2.0 / math.pi)
    for j in range(D_FF // ch):
        c0 = j * ch
        gext[...] = _dot(hext[...], wup_ref[:, D_FF + c0:D_FF + c0 + ch])
        gc = (dw_ref[0:1, c0:c0 + ch] * gext[HALO - 1:HALO - 1 + ts, :]
              + dw_ref[1:2, c0:c0 + ch] * gext[HALO:HALO + ts, :]
              + dw_ref[2:3, c0:c0 + ch] * gext[HALO + 1:HALO + 1 + ts, :])
        up = _dot(hext[HALO:HALO + ts, :], wup_ref[:, c0:c0 + ch])
        cdf = 0.5 * (1.0 + jnp.tanh(c_gelu * (gc + 0.044715 * (gc * gc * gc))))
        z[:, c0:c0 + ch] = (up * (gc * cdf)).astype(bf16)
    y = _dot(z[...], wdn_ref[...])
    y_ref[0] = x + gate * y


def _ffn_kernel(*refs, ts, ch):
    _ffn_body(pl.program_id(1), pl.num_programs(1) - 1, *refs, ts=ts, ch=ch)


def _ffn_io(x, modl, g2, w_up_b, ffn_dw, w_down_b, l, ts, ch, at):
    s = x.shape[1]
    const = at(lambda bi, i: (0, 0))
    tile = pl.BlockSpec((1, ts, D_MODEL), at(lambda bi, i: (bi, i, 0)))
    single = pl.Buffered(1)
    in_specs = [
        tile,
        *_halo_specs(D_MODEL, s, ts, at),
        pl.BlockSpec((1, 1, 6 * D_MODEL), at(lambda bi, i: (bi, 0, 0))),
        pl.BlockSpec((1, D_MODEL), const),
        pl.BlockSpec((None, D_MODEL, 2 * D_FF), at(lambda bi, i: (l, 0, 0)), pipeline_mode=single),
        pl.BlockSpec((3, D_FF), const),
        pl.BlockSpec((None, D_FF, D_MODEL), at(lambda bi, i: (l, 0, 0)), pipeline_mode=single),
    ]
    scratch = [pltpu.VMEM((ts + 2 * HALO, D_MODEL), bf16),
               pltpu.VMEM((ts + 2 * HALO, ch), f32),
               pltpu.VMEM((ts, D_FF), bf16)]
    args = (x, x, x, modl, g2, w_up_b, ffn_dw, w_down_b)
    return args, in_specs, tile, scratch


def _ffn(x, ffn_args, ts=1024, ch=256):
    b, s, _ = x.shape
    args, in_specs, out_spec, scratch = _ffn_io(x, *ffn_args, ts, ch, lambda f: f)
    return pl.pallas_call(
        functools.partial(_ffn_kernel, ts=ts, ch=ch),
        out_shape=jax.ShapeDtypeStruct(x.shape, f32),
        grid=(b, s // ts),
        in_specs=in_specs,
        out_specs=out_spec,
        scratch_shapes=scratch,
        compiler_params=_params(("parallel", "parallel")),
        name="ffn",
    )(*args)


def _pre_mix(x, modl, tables, p, l):
    outs = _inproj(x, modl, p["norm1_g"][l][None], p["w_in"], l, p["e_mat"],
                   p["q_g"][l][None], p["k_g"][l][None])
    ol = []
    for pi, (_, d) in enumerate(PATTERNS):
        qkv = outs[pi][:, None] if d == 1 else outs[pi]
        ol += _attention(qkv, tables[pi], d)
    return (modl, ol, outs[3],
            jnp.broadcast_to(p["conv_dw_w"][l][:, None, :], (CONV_KERNEL, SUBLANES, D_CONV)),
            jnp.broadcast_to(p["conv_dw_b"][l][None], (SUBLANES, D_CONV)),
            p["conv_ln_g"][l][None], p["conv_ln_b"][l][None], p["w_out"], l)


def _ffn_operands(modl, p, l):
    return (modl, p["norm2_g"][l][None], p["w_up"], p["ffn_dw_w"][l], p["w_down"], l)


def _trunk(x, mod, tables, p):
    for l in range(mod.shape[0]):
        modl = mod[l][:, None, :]
        x = _mix(x, _pre_mix(x, modl, tables, p, l))
        x = _ffn(x, _ffn_operands(modl, p, l))
    return x


def kernel(x_prompt, x_sample, c_prompt, c_sample, rel_bias, norm1_g, norm2_g, w_ada, b_ada, w_in, q_norm_g, k_norm_g, conv_dw_w, conv_dw_b, conv_ln_g, conv_ln_b, w_out, w_up, ffn_dw_w, w_down):
    nb = c_prompt.shape[0]
    mod = _modulation(jnp.concatenate([c_prompt, c_sample], axis=0), w_ada, b_ada)
    head_id = np.arange(D_ATT) // HEAD_DIM
    e_mat = jnp.asarray((head_id[:, None] == head_id[None, :]) / HEAD_DIM, dtype=bf16)
    p = dict(
        norm1_g=norm1_g, norm2_g=norm2_g,
        w_in=w_in.astype(bf16), w_out=w_out.astype(bf16),
        w_up=w_up.astype(bf16), w_down=w_down.astype(bf16),
        q_g=jnp.tile(q_norm_g, (1, N_HEADS)) * (ATT_SCALE * LOG2E), k_g=jnp.tile(k_norm_g, (1, N_HEADS)),
        conv_dw_w=conv_dw_w, conv_dw_b=conv_dw_b, conv_ln_g=conv_ln_g, conv_ln_b=conv_ln_b,
        ffn_dw_w=ffn_dw_w, e_mat=e_mat,
    )
    tables = [[_bias_table(rel_bias, d, min(2 * QT, x.shape[1] // d)) for _, d in PATTERNS]
              for x in (x_prompt, x_sample)]
    return (_trunk(x_prompt, mod[:, :nb], tables[0], p), _trunk(x_sample, mod[:, nb:], tables[1], p))
```

```python
import functools
import math

import numpy as np
import jax
import jax.numpy as jnp
from jax import lax
from jax.experimental import pallas as pl
from jax.experimental.pallas import tpu as pltpu

D_MODEL = 1024
HEAD_DIM = 64
N_HEADS = 8
D_ATT = N_HEADS * HEAD_DIM
D_CONV = D_MODEL - D_ATT
D_IN = 3 * D_ATT + 2 * D_CONV
CONV_KERNEL = 31
D_FF = 2816
PATTERNS = ((128, 1), (512, 4), (2048, 16))
SIDE = 64
N_BUCKETS = 32
REL_MAX_DIST = 1024
EPS = 1e-6
NEG = -1e30
ATT_SCALE = 1.0 / math.sqrt(HEAD_DIM)
LOG2E = 1.0 / math.log(2.0)
LN2 = math.log(2.0)

LANES = 128
SUBLANES = 8
QT = 128
STEP_TOKENS = 2048
HALO = 16
VMEM_LIMIT = 56 * 1024 * 1024

f32 = jnp.float32
bf16 = jnp.bfloat16


def _dot(a, b):
    return jnp.dot(a, b, preferred_element_type=f32)


def _sigmoid(x):
    return 0.5 * jnp.tanh(0.5 * x) + 0.5


def _split_bf16(x):
    hi = x.astype(bf16)
    lo = (x - hi.astype(f32)).astype(bf16)
    return hi, lo


def _params(sem):
    return pltpu.CompilerParams(dimension_semantics=sem, vmem_limit_bytes=VMEM_LIMIT)


def _mod_kernel(c_ref, w_ref, b_ref, o_ref):
    c = c_ref[...]
    sc_hi, sc_lo = _split_bf16(c * _sigmoid(c))
    w_hi, w_lo = _split_bf16(w_ref[0])
    acc = _dot(sc_hi, w_hi) + _dot(sc_hi, w_lo) + _dot(sc_lo, w_hi)
    o_ref[0] = acc + b_ref[0]


def _modulation(c_all, w_ada, b_ada):
    depth, _, n_out = w_ada.shape
    rows = c_all.shape[0]
    nb = 1536
    return pl.pallas_call(
        _mod_kernel,
        out_shape=jax.ShapeDtypeStruct((depth, rows, n_out), f32),
        grid=(depth, n_out // nb),
        in_specs=[
            pl.BlockSpec((rows, D_MODEL), lambda l, j: (0, 0)),
            pl.BlockSpec((1, D_MODEL, nb), lambda l, j: (l, 0, j)),
            pl.BlockSpec((1, 1, nb), lambda l, j: (l, 0, j)),
        ],
        out_specs=pl.BlockSpec((1, rows, nb), lambda l, j: (l, 0, j)),
        compiler_params=_params(("arbitrary", "arbitrary")),
        name="adaln_mod",
    )(c_all, w_ada, b_ada.reshape(depth, 1, n_out))


def _inproj_kernel(x_ref, mod_ref, g1_ref, w_ref, e_ref, qg_ref, kg_ref,
                   qkv1_ref, qkv4_ref, qkv16_ref, u_ref, scr, scr4, *, ts):
    x = x_ref[0]
    ms = jnp.mean(x * x, axis=-1, keepdims=True)
    shift = mod_ref[0, :, 0:D_MODEL]
    scale = mod_ref[0, :, D_MODEL:2 * D_MODEL]
    h = (x * lax.rsqrt(ms + EPS)) * (g1_ref[...] * (1.0 + scale)) + shift
    proj = _dot(h.astype(bf16), w_ref[...])

    lo_half = lax.broadcasted_iota(jnp.int32, (1, LANES), 1) < HEAD_DIM

    def head_norm(t, g):
        parts = []
        for s in range(D_ATT // LANES):
            sq = t[:, s * LANES:(s + 1) * LANES]
            sq = sq * sq
            lo = jnp.sum(jnp.where(lo_half, sq, 0.0), axis=-1, keepdims=True)
            hi = jnp.sum(jnp.where(lo_half, 0.0, sq), axis=-1, keepdims=True)
            parts.append(jnp.where(lo_half, lo, hi))
        msh = jnp.concatenate(parts, axis=1) * (1.0 / HEAD_DIM)
        return t * lax.rsqrt(msh + EPS) * g

    q = head_norm(proj[:, 0:D_ATT], qg_ref[...])
    k = head_norm(proj[:, D_ATT:2 * D_ATT], kg_ref[...])
    v = proj[:, 2 * D_ATT:3 * D_ATT]
    cv = proj[:, 3 * D_ATT:3 * D_ATT + D_CONV]
    cg = proj[:, 3 * D_ATT + D_CONV:]
    u_ref[0] = cv * _sigmoid(cg)

    n_slab = D_ATT // LANES
    for a, t in enumerate((q, k, v)):
        qkv1_ref[0, :, a * D_ATT:(a + 1) * D_ATT] = t.astype(bf16)
        for s in range(n_slab):
            scr[a * n_slab + s] = t[:, s * LANES:(s + 1) * LANES]
    n4, n16 = ts // 4, ts // 16
    for a in range(3):
        for s in range(n_slab):
            cols = slice(a * D_ATT + s * LANES, a * D_ATT + (s + 1) * LANES)
            for r4 in range(4):
                part = scr[a * n_slab + s, pl.ds(r4, n4, stride=4), :]
                qkv4_ref[0, r4, :, cols] = part.astype(bf16)
                scr4[a * n_slab + s, r4 * n4:(r4 + 1) * n4, :] = part
            for r16 in range(16):
                part = scr4[a * n_slab + s, pl.ds((r16 % 4) * n4 + r16 // 4, n16, stride=4), :]
                qkv16_ref[0, r16, :, cols] = part.astype(bf16)


def _inproj(x, modl, g1, w_in_b, l, e_mat, qg, kg, ts=512):
    b, s, _ = x.shape
    nt = s // ts
    out_shape = [jax.ShapeDtypeStruct((b, s, 3 * D_ATT), bf16)]
    out_specs = [pl.BlockSpec((1, ts, 3 * D_ATT), lambda bi, i: (bi, i, 0))]
    for d in (4, 16):
        out_shape.append(jax.ShapeDtypeStruct((b, d, s // d, 3 * D_ATT), bf16))
        out_specs.append(pl.BlockSpec((1, d, ts // d, 3 * D_ATT), lambda bi, i: (bi, 0, i, 0)))
    out_shape.append(jax.ShapeDtypeStruct((b, s, D_CONV), f32))
    out_specs.append(pl.BlockSpec((1, ts, D_CONV), lambda bi, i: (bi, i, 0)))
    const = lambda bi, i: (0, 0)
    return pl.pallas_call(
        functools.partial(_inproj_kernel, ts=ts),
        out_shape=out_shape,
        grid=(b, nt),
        in_specs=[
            pl.BlockSpec((1, ts, D_MODEL), lambda bi, i: (bi, i, 0)),
            pl.BlockSpec((1, 1, 6 * D_MODEL), lambda bi, i: (bi, 0, 0)),
            pl.BlockSpec((1, D_MODEL), const),
            pl.BlockSpec((None, D_MODEL, D_IN), lambda bi, i: (l, 0, 0)),
            pl.BlockSpec((D_ATT, D_ATT), const),
            pl.BlockSpec((1, D_ATT), const),
            pl.BlockSpec((1, D_ATT), const),
        ],
        out_specs=out_specs,
        scratch_shapes=[pltpu.VMEM((3 * D_ATT // LANES, ts, LANES), f32)] * 2,
        compiler_params=_params(("parallel", "parallel")),
        name="inproj",
    )(x, modl, g1, w_in_b, e_mat, qg, kg)


def _attn_kernel(q_ref, k_ref, v_ref, tbl_ref, o_ref, l_ref, p_a, p_b, m_a, m_b,
                 *, d, g_tiles, seq, win):
    i = pl.program_id(1)
    n_tiles = seq // QT
    n_it = d * g_tiles
    lane = lax.broadcasted_iota(jnp.int32, (1, LANES), 1)
    lo_half = lane < HEAD_DIM
    halves = (lo_half, jnp.logical_not(lo_half))
    n_pair = D_ATT // LANES

    def coords(it):
        r = it // g_tiles
        g = it % g_tiles
        t = i * g_tiles + g
        start = pl.multiple_of(jnp.clip(t * QT - SIDE, 0, seq - win), SIDE)
        var = jnp.where(t == 0, 0, jnp.where(t == n_tiles - 1, 2, 1))
        row0 = pl.multiple_of(g * QT, QT)
        return r, start, var, row0

    def scores(it, p_scr, m_scr):
        r, start, var, row0 = coords(it)
        for hp in range(n_pair):
            cols = slice(hp * LANES, (hp + 1) * LANES)
            qp = q_ref[0, r, pl.ds(row0, QT), cols]
            kp = k_ref[0, r, pl.ds(start, win), cols]
            mx = []
            for e in range(2):
                qm = jnp.where(halves[e], qp, jnp.zeros_like(qp))
                s = lax.dot_general(qm, kp, (((1,), (1,)), ((), ())), preferred_element_type=f32)
                s = s + tbl_ref[var, 2 * hp + e]
                m = jnp.max(s, axis=-1, keepdims=True)
                p_scr[2 * hp + e] = jnp.exp2(s - m).astype(bf16)
                mx.append(m)
            m_scr[hp] = jnp.where(lo_half, mx[0], mx[1])

    def values(it, p_scr, m_scr):
        r, start, var, row0 = coords(it)
        for hp in range(n_pair):
            cols = slice(hp * LANES, (hp + 1) * LANES)
            vp = v_ref[0, r, pl.ds(start, win), cols]
            pv = [_dot(p_scr[2 * hp + e], jnp.where(halves[e], vp, jnp.ones_like(vp)))
                  for e in range(2)]
            num = jnp.where(lo_half, pv[0], pv[1])
            den = pltpu.roll(jnp.where(lo_half, pv[1], pv[0]), HEAD_DIM, axis=1)
            if d == 1:
                rows = pl.ds(row0, QT)
            else:
                rows = pl.ds(row0 * d + r, QT, stride=d)
            o_ref[0, hp, rows, :] = num / den
            l_ref[0, hp, rows, :] = m_scr[hp] * LN2 + jnp.log(den)

    scores(0, p_a, m_a)

    def body(j, carry):
        scores(2 * j + 1, p_b, m_b)
        values(2 * j, p_a, m_a)
        scores(2 * j + 2, p_a, m_a)
        values(2 * j + 1, p_b, m_b)
        return carry

    lax.fori_loop(0, n_it // 2 - 1, body, 0)
    scores(n_it - 1, p_b, m_b)
    values(n_it - 2, p_a, m_a)
    values(n_it - 1, p_b, m_b)


def _attention(qkv, tbl, d):
    b, _, seq, _ = qkv.shape
    s_nat = seq * d
    g_tiles = STEP_TOKENS // (QT * d)
    win = tbl.shape[-1]
    n_slab = D_ATT // LANES
    out = jax.ShapeDtypeStruct((b, n_slab, s_nat, LANES), f32)
    out_spec = pl.BlockSpec((1, n_slab, STEP_TOKENS, LANES), lambda bi, i: (bi, 0, i, 0))
    full = lambda blk: pl.BlockSpec((1, d, seq, D_ATT), lambda bi, i: (bi, 0, 0, blk))
    p_slot = pltpu.VMEM((N_HEADS, QT, win), bf16)
    m_slot = pltpu.VMEM((n_slab, QT, LANES), f32)
    return pl.pallas_call(
        functools.partial(_attn_kernel, d=d, g_tiles=g_tiles, seq=seq, win=win),
        out_shape=[out, out],
        grid=(b, s_nat // STEP_TOKENS),
        in_specs=[
            pl.BlockSpec((1, d, QT * g_tiles, D_ATT), lambda bi, i: (bi, 0, i, 0)),
            full(1),
            full(2),
            pl.BlockSpec(tbl.shape, lambda bi, i: (0, 0, 0, 0)),
        ],
        out_specs=[out_spec, out_spec],
        scratch_shapes=[p_slot, p_slot, m_slot, m_slot],
        compiler_params=_params(("parallel", "arbitrary")),
        name=f"attn_d{d}",
    )(qkv, qkv, qkv, tbl)


def _t5_bucket(rel):
    n = -rel
    half = N_BUCKETS // 2
    ret = (n < 0).astype(np.int32) * half
    n = np.abs(n)
    max_exact = half // 2
    large = max_exact + (np.log(np.maximum(n, 1) / max_exact) / np.log(REL_MAX_DIST / max_exact)
                         * (half - max_exact)).astype(np.int32)
    large = np.minimum(large, half - 1)
    return (ret + np.where(n < max_exact, n, large)).astype(np.int32)


def _table_kernel(bias_ref, idx_ref, o_ref):
    h = pl.program_id(1)
    idx = idx_ref[0]
    t = jnp.full(idx.shape, NEG, f32)
    for bucket in range(N_BUCKETS):
        t = jnp.where(idx == bucket, bias_ref[bucket, h], t)
    o_ref[0, 0] = t * LOG2E


def _bias_table(rel_bias, d, win):
    t = np.arange(QT)[None, :, None]
    u = np.arange(win)[None, None, :]
    offset = np.array([0, SIDE, 2 * SIDE])[:, None, None]
    rel = u - t - offset
    idx = np.where(np.abs(rel) <= SIDE, _t5_bucket(np.clip(rel, -SIDE, SIDE) * d), -1)
    return pl.pallas_call(
        _table_kernel,
        out_shape=jax.ShapeDtypeStruct((3, N_HEADS, QT, win), f32),
        grid=(3, N_HEADS),
        in_specs=[
            pl.BlockSpec(memory_space=pltpu.SMEM),
            pl.BlockSpec((1, QT, win), lambda v, h: (v, 0, 0)),
        ],
        out_specs=pl.BlockSpec((1, 1, QT, win), lambda v, h: (v, h, 0, 0)),
        compiler_params=_params(("arbitrary", "arbitrary")),
        name=f"bias_table_d{d}",
    )(rel_bias, jnp.asarray(idx, jnp.int32))


def _mix_body(i, last, x_ref, mod_ref, o1_ref, l1_ref, o2_ref, l2_ref, o3_ref, l3_ref,
              u_ref, up_ref, un_ref, dww_ref, dwb_ref, lng_ref, lnb_ref, wout_ref,
              y_ref, uext, conv, cat, *, ts, rc):
    for s in range(D_ATT // LANES):
        la, lb, lc = l1_ref[0, s], l2_ref[0, s], l3_ref[0, s]
        m = jnp.maximum(jnp.maximum(la, lb), lc)
        ea, eb, ec = jnp.exp(la - m), jnp.exp(lb - m), jnp.exp(lc - m)
        att = (ea * o1_ref[0, s] + eb * o2_ref[0, s] + ec * o3_ref[0, s]) / (ea + eb + ec)
        cat[:, s * LANES:(s + 1) * LANES] = att.astype(bf16)

    uext[0, 0:HALO, :] = jnp.where(i > 0, up_ref[0], 0.0)
    uext[0, HALO:HALO + ts, :] = u_ref[0]
    uext[0, HALO + ts:2 * HALO + ts, :] = jnp.where(i < last, un_ref[0], 0.0)
    n_sh = ts + 2 * HALO - SUBLANES
    for sh in range(1, SUBLANES):
        uext[sh, 0:n_sh, :] = uext[0, sh:sh + n_sh, :]
    pad = HALO - CONV_KERNEL // 2

    def chunk(c, carry):
        base = pl.multiple_of(c * rc, rc)
        acc = [dwb_ref[...]] * (rc // SUBLANES)
        for k in range(CONV_KERNEL):
            off = k + pad
            w = dww_ref[k]
            row = base + (off // SUBLANES) * SUBLANES
            for g in range(rc // SUBLANES):
                acc[g] = acc[g] + w * uext[off % SUBLANES, pl.ds(row + g * SUBLANES, SUBLANES), :]
        conv[pl.ds(base, rc), :] = jnp.concatenate(acc, axis=0)
        return carry

    lax.fori_loop(0, ts // rc, chunk, 0)
    cv = conv[...]
    mu = jnp.mean(cv, axis=-1, keepdims=True)
    xc = cv - mu
    var = jnp.mean(xc * xc, axis=-1, keepdims=True)
    yn = xc * lax.rsqrt(var + EPS) * lng_ref[...] + lnb_ref[...]
    cat[:, D_ATT:] = (yn * _sigmoid(yn)).astype(bf16)
    y = _dot(cat[...], wout_ref[...])
    gate = mod_ref[0, :, 2 * D_MODEL:3 * D_MODEL]
    y_ref[0] = x_ref[0] + gate * y


def _mix_kernel(*refs, ts, rc):
    _mix_body(pl.program_id(1), pl.num_programs(1) - 1, *refs, ts=ts, rc=rc)


def _halo_specs(width, s, ts, at):
    hb = ts // HALO
    prev = pl.BlockSpec((1, HALO, width), at(lambda bi, i: (bi, jnp.maximum(i * hb - 1, 0), 0)))
    nxt = pl.BlockSpec((1, HALO, width),
                       at(lambda bi, i: (bi, jnp.minimum((i + 1) * hb, s // HALO - 1), 0)))
    return prev, nxt


def _mix_io(x, modl, ol, u, dww, dwb, lng, lnb, w_out_b, l, ts, at):
    s = x.shape[1]
    n_slab = D_ATT // LANES
    slab = pl.BlockSpec((1, n_slab, ts, LANES), at(lambda bi, i: (bi, 0, i, 0)))
    const = at(lambda bi, i: (0, 0))
    tile = lambda w: pl.BlockSpec((1, ts, w), at(lambda bi, i: (bi, i, 0)))
    in_specs = [
        tile(D_MODEL),
        pl.BlockSpec((1, 1, 6 * D_MODEL), at(lambda bi, i: (bi, 0, 0))),
        slab, slab, slab, slab, slab, slab,
        tile(D_CONV),
        *_halo_specs(D_CONV, s, ts, at),
        pl.BlockSpec((CONV_KERNEL, SUBLANES, D_CONV), at(lambda bi, i: (0, 0, 0))),
        pl.BlockSpec((SUBLANES, D_CONV), const),
        pl.BlockSpec((1, D_CONV), const),
        pl.BlockSpec((1, D_CONV), const),
        pl.BlockSpec((None, D_MODEL, D_MODEL), at(lambda bi, i: (l, 0, 0)),
                     pipeline_mode=pl.Buffered(1)),
    ]
    scratch = [pltpu.VMEM((SUBLANES, ts + 2 * HALO, D_CONV), f32),
               pltpu.VMEM((ts, D_CONV), f32),
               pltpu.VMEM((ts, D_MODEL), bf16)]
    args = (x, modl, *ol, u, u, u, dww, dwb, lng, lnb, w_out_b)
    return args, in_specs, tile(D_MODEL), scratch


def _mix(x, mix_args, ts=512, rc=32):
    b, s, _ = x.shape
    args, in_specs, out_spec, scratch = _mix_io(x, *mix_args, ts, lambda f: f)
    return pl.pallas_call(
        functools.partial(_mix_kernel, ts=ts, rc=rc),
        out_shape=jax.ShapeDtypeStruct(x.shape, f32),
        grid=(b, s // ts),
        in_specs=in_specs,
        out_specs=out_spec,
        scratch_shapes=scratch,
        compiler_params=_params(("parallel", "parallel")),
        name="mix",
    )(*args)


def _ffn_body(i, last, x_ref, xp_ref, xn_ref, mod_ref, g2_ref, wup_ref, dw_ref, wdn_ref,
              y_ref, hext, gext, z, *, ts, ch):
    shift = mod_ref[0, :, 3 * D_MODEL:4 * D_MODEL]
    scale = mod_ref[0, :, 4 * D_MODEL:5 * D_MODEL]
    gate = mod_ref[0, :, 5 * D_MODEL:6 * D_MODEL]
    a = g2_ref[...] * (1.0 + scale)

    def norm(xx):
        ms = jnp.mean(xx * xx, axis=-1, keepdims=True)
        return (xx * lax.rsqrt(ms + EPS)) * a + shift

    x = x_ref[0]
    hext[0:HALO, :] = jnp.where(i > 0, norm(xp_ref[0]), 0.0).astype(bf16)
    hext[HALO:HALO + ts, :] = norm(x).astype(bf16)
    hext[HALO + ts:2 * HALO + ts, :] = jnp.where(i < last, norm(xn_ref[0]), 0.0).astype(bf16)

    c_gelu = math.sqrt(2.0 / math.pi)
    for j in range(D_FF // ch):
        c0 = j * ch
        gext[...] = _dot(hext[...], wup_ref[:, D_FF + c0:D_FF + c0 + ch])
        gc = (dw_ref[0:1, c0:c0 + ch] * gext[HALO - 1:HALO - 1 + ts, :]
              + dw_ref[1:2, c0:c0 + ch] * gext[HALO:HALO + ts, :]
              + dw_ref[2:3, c0:c0 + ch] * gext[HALO + 1:HALO + 1 + ts, :])
        up = _dot(hext[HALO:HALO + ts, :], wup_ref[:, c0:c0 + ch])
        cdf = 0.5 * (1.0 + jnp.tanh(c_gelu * (gc + 0.044715 * (gc * gc * gc))))
        z[:, c0:c0 + ch] = (up * (gc * cdf)).astype(bf16)
    y = _dot(z[...], wdn_ref[...])
    y_ref[0] = x + gate * y


def _ffn_kernel(*refs, ts, ch):
    _ffn_body(pl.program_id(1), pl.num_programs(1) - 1, *refs, ts=ts, ch=ch)


def _ffn_io(x, modl, g2, w_up_b, ffn_dw, w_down_b, l, ts, ch, at):
    s = x.shape[1]
    const = at(lambda bi, i: (0, 0))
    tile = pl.BlockSpec((1, ts, D_MODEL), at(lambda bi, i: (bi, i, 0)))
    single = pl.Buffered(1)
    in_specs = [
        tile,
        *_halo_specs(D_MODEL, s, ts, at),
        pl.BlockSpec((1, 1, 6 * D_MODEL), at(lambda bi, i: (bi, 0, 0))),
        pl.BlockSpec((1, D_MODEL), const),
        pl.BlockSpec((None, D_MODEL, 2 * D_FF), at(lambda bi, i: (l, 0, 0)), pipeline_mode=single),
        pl.BlockSpec((3, D_FF), const),
        pl.BlockSpec((None, D_FF, D_MODEL), at(lambda bi, i: (l, 0, 0)), pipeline_mode=single),
    ]
    scratch = [pltpu.VMEM((ts + 2 * HALO, D_MODEL), bf16),
               pltpu.VMEM((ts + 2 * HALO, ch), f32),
               pltpu.VMEM((ts, D_FF), bf16)]
    args = (x, x, x, modl, g2, w_up_b, ffn_dw, w_down_b)
    return args, in_specs, tile, scratch


def _ffn(x, ffn_args, ts=1024, ch=256):
    b, s, _ = x.shape
    args, in_specs, out_spec, scratch = _ffn_io(x, *ffn_args, ts, ch, lambda f: f)
    return pl.pallas_call(
        functools.partial(_ffn_kernel, ts=ts, ch=ch),
        out_shape=jax.ShapeDtypeStruct(x.shape, f32),
        grid=(b, s // ts),
        in_specs=in_specs,
        out_specs=out_spec,
        scratch_shapes=scratch,
        compiler_params=_params(("parallel", "parallel")),
        name="ffn",
    )(*args)


def _pre_mix(x, modl, tables, p, l):
    outs = _inproj(x, modl, p["norm1_g"][l][None], p["w_in"], l, p["e_mat"],
                   p["q_g"][l][None], p["k_g"][l][None])
    ol = []
    for pi, (_, d) in enumerate(PATTERNS):
        qkv = outs[pi][:, None] if d == 1 else outs[pi]
        ol += _attention(qkv, tables[pi], d)
    return (modl, ol, outs[3],
            jnp.broadcast_to(p["conv_dw_w"][l][:, None, :], (CONV_KERNEL, SUBLANES, D_CONV)),
            jnp.broadcast_to(p["conv_dw_b"][l][None], (SUBLANES, D_CONV)),
            p["conv_ln_g"][l][None], p["conv_ln_b"][l][None], p["w_out"], l)


def _ffn_operands(modl, p, l):
    return (modl, p["norm2_g"][l][None], p["w_up"], p["ffn_dw_w"][l], p["w_down"], l)


def _trunk(x, mod, tables, p):
    for l in range(mod.shape[0]):
        modl = mod[l][:, None, :]
        x = _mix(x, _pre_mix(x, modl, tables, p, l))
        x = _ffn(x, _ffn_operands(modl, p, l))
    return x


def kernel(x_prompt, x_sample, c_prompt, c_sample, rel_bias, norm1_g, norm2_g, w_ada, b_ada, w_in, q_norm_g, k_norm_g, conv_dw_w, conv_dw_b, conv_ln_g, conv_ln_b, w_out, w_up, ffn_dw_w, w_down):
    nb = c_prompt.shape[0]
    mod = _modulation(jnp.concatenate([c_prompt, c_sample], axis=0), w_ada, b_ada)
    head_id = np.arange(D_ATT) // HEAD_DIM
    e_mat = jnp.asarray((head_id[:, None] == head_id[None, :]) / HEAD_DIM, dtype=bf16)
    p = dict(
        norm1_g=norm1_g, norm2_g=norm2_g,
        w_in=w_in.astype(bf16), w_out=w_out.astype(bf16),
        w_up=w_up.astype(bf16), w_down=w_down.astype(bf16),
        q_g=jnp.tile(q_norm_g, (1, N_HEADS)) * (ATT_SCALE * LOG2E), k_g=jnp.tile(k_norm_g, (1, N_HEADS)),
        conv_dw_w=conv_dw_w, conv_dw_b=conv_dw_b, conv_ln_g=conv_ln_g, conv_ln_b=conv_ln_b,
        ffn_dw_w=ffn_dw_w, e_mat=e_mat,
    )
    tables = [[_bias_table(rel_bias, d, min(2 * QT, x.shape[1] // d)) for _, d in PATTERNS]
              for x in (x_prompt, x_sample)]
    return (_trunk(x_prompt, mod[:, :nb], tables[0], p), _trunk(x_sample, mod[:, nb:], tables[1], p))
```
